```python
import jax, jax.numpy as jnp
from jax import lax
import numpy as np

D_MODEL = 2048
BATCH = 4
SEQ = 2048
DEPTH = 4

N_EVEN = (DEPTH + 1) // 2
N_ODD = DEPTH // 2
D_FF = 5632
NORM_EPS = 1e-6
LN_EPS = 1e-5
POOL_DIM = D_MODEL // 2
POOL_WINDOWS = (2, 4, 8, 16)
N_POOL_GROUPS = len(POOL_WINDOWS)
POOL_GROUP_DIM = POOL_DIM // N_POOL_GROUPS
RW_DIM = D_MODEL // 2
RW_HEAD = 64
RW_HEADS = RW_DIM // RW_HEAD
RW_DECAY_LORA = 64
RW_A_LORA = 64
RW_GATE_LORA = 160
RW_IN = 3 * RW_DIM + RW_DECAY_LORA + RW_A_LORA + RW_GATE_LORA
EV_IN = POOL_DIM + RW_IN
RW_GN_EPS = 64e-5
CV_DIM = D_MODEL // 2
CV_WIDTH = 31
SG_DIM = D_MODEL // 2
SG_CHUNK = 128
SG_GROUPS = 8
SG_GROUP_DIM = SG_DIM // SG_GROUPS
OD_IN = 2 * CV_DIM + 2 * SG_DIM

kernel_name = 'hybrid_pool_rwkv7_conv_gmlp_macaron'


def rms_norm(x, g):
    xf = x.astype(jnp.float32)
    y = xf * lax.rsqrt(jnp.mean(jnp.square(xf), -1, keepdims=True) + NORM_EPS)
    return (y * g).astype(x.dtype)


def layer_norm(x, g, b):
    xf = x.astype(jnp.float32)
    mu = jnp.mean(xf, -1, keepdims=True)
    var = jnp.mean(jnp.square(xf - mu), -1, keepdims=True)
    return (xf - mu) * lax.rsqrt(var + LN_EPS) * g + b


def swiglu(x, w1, w3, w2):
    return (jax.nn.silu(x @ w1) * (x @ w3)) @ w2


def multiscale_pool(p, w_grp, scale):
    B, T, _ = p.shape
    pg = p.astype(jnp.float32).reshape(B, T, N_POOL_GROUPS, POOL_GROUP_DIM)
    cs = jnp.cumsum(pg, axis=1)
    t = jnp.arange(T)
    means = []
    for gi, w in enumerate(POOL_WINDOWS):
        c = cs[:, :, gi]
        c_prev = jnp.pad(c, ((0, 0), (w, 0), (0, 0)))[:, :T]
        cnt = jnp.minimum(t + 1, w).astype(jnp.float32)
        means.append((c - c_prev) / cnt[None, :, None])
    d = jnp.stack(means, axis=2) - pg
    out = jnp.einsum('btgc,gcd->btgd', d, w_grp)
    return out.reshape(B, T, POOL_DIM) * scale


def wkv7_scan(r, decay, k, v, a_vec, b_vec):
    B, T, H, N = r.shape

    def step(S, inp):
        r_t, w_t, k_t, v_t, a_t, b_t = inp
        sa = jnp.einsum('bhij,bhj->bhi', S, a_t)
        S = S * w_t[:, :, None, :] + sa[..., None] * b_t[:, :, None, :] + v_t[..., None] * k_t[:, :, None, :]
        y = jnp.einsum('bhij,bhj->bhi', S, r_t)
        return S, y

    xs = tuple(jnp.moveaxis(z, 1, 0) for z in (r, decay, k, v, a_vec, b_vec))
    S0 = jnp.zeros((B, H, N, N), jnp.float32)
    _, y = lax.scan(step, S0, xs)
    return jnp.moveaxis(y, 0, 1)


def rwkv7_mix(h, mu, w0, w2, a0, a2, g2, k_k, k_a, r_k, ln_g, ln_b):
    B, T, _ = h.shape
    h = h.astype(jnp.float32)
    prev = jnp.pad(h, ((0, 0), (1, 0), (0, 0)))[:, :T]
    h = h + (prev - h) * mu
    cuts = [RW_DIM, 2 * RW_DIM, 3 * RW_DIM, 3 * RW_DIM + RW_DECAY_LORA, 3 * RW_DIM + RW_DECAY_LORA + RW_A_LORA]
    r, k, v, hw, ha, hg = jnp.split(h, cuts, axis=-1)
    w_log = -jax.nn.softplus(-(w0 + jnp.tanh(hw) @ w2)) - 0.5
    decay = jnp.exp(-jnp.exp(w_log))
    a = jax.nn.sigmoid(a0 + ha @ a2)
    g = jax.nn.sigmoid(hg) @ g2
    heads = lambda z: z.reshape(B, T, RW_HEADS, RW_HEAD)
    kk = heads(k * k_k)
    kk = kk / jnp.maximum(jnp.sqrt(jnp.sum(jnp.square(kk), -1, keepdims=True)), 1e-12)
    k = k * (1.0 + (a - 1.0) * k_a)
    rh, kh, vh = heads(r), heads(k), heads(v)
    y = wkv7_scan(rh, heads(decay), kh, vh, -kk, kk * heads(a))
    ym = jnp.mean(y, -1, keepdims=True)
    yv = jnp.mean(jnp.square(y - ym), -1, keepdims=True)
    y = ((y - ym) * lax.rsqrt(yv + RW_GN_EPS)).reshape(B, T, RW_DIM) * ln_g + ln_b
    bonus = jnp.sum(rh * kh * r_k, -1, keepdims=True) * vh
    y = y + bonus.reshape(B, T, RW_DIM)
    return y * g


def conformer_conv(c, dw, db, ln_g, ln_b):
    c = c.astype(jnp.float32)
    a, gate = jnp.split(c, 2, axis=-1)
    y = a * jax.nn.sigmoid(gate)
    y = lax.conv_general_dilated(y, dw.astype(jnp.float32)[:, None, :], (1,), ((CV_WIDTH - 1, 0),),
                                 dimension_numbers=('NWC', 'WIO', 'NWC'), feature_group_count=CV_DIM) + db
    return jax.nn.silu(layer_norm(y, ln_g, ln_b))


def chunked_spatial_gate(z, ln_g, ln_b, w_s, b_s):
    z = jax.nn.gelu(z.astype(jnp.float32), approximate=False)
    u, v = jnp.split(z, 2, axis=-1)
    v = layer_norm(v, ln_g, ln_b)
    B, T, _ = v.shape
    vc = v.reshape(B, T // SG_CHUNK, SG_CHUNK, SG_GROUPS, SG_GROUP_DIM)
    mask = jnp.tril(jnp.ones((SG_CHUNK, SG_CHUNK), bool))
    ws = jnp.where(mask[None], w_s, 0.0)
    s = jnp.einsum('gij,bcjgd->bcigd', ws, vc) + b_s.T[None, None, :, :, None]
    return u * s.reshape(B, T, SG_DIM)


def setup_inputs(seed: int = 0) -> dict:
    key = jax.random.key(seed)
    ks = jax.random.split(key, 32)
    f32 = jnp.float32
    nrm = lambda k, shape, scale: jax.random.normal(k, shape, f32) * scale
    return {
        'x': nrm(ks[0], (BATCH, SEQ, D_MODEL), 1.0),
        'norm_g': 1.0 + nrm(ks[1], (DEPTH, 6, D_MODEL), 0.02),
        'ffn_w1': nrm(ks[2], (DEPTH, 2, D_MODEL, D_FF), D_MODEL ** -0.5),
        'ffn_w3': nrm(ks[3], (DEPTH, 2, D_MODEL, D_FF), D_MODEL ** -0.5),
        'ffn_w2': nrm(ks[4], (DEPTH, 2, D_FF, D_MODEL), D_FF ** -0.5),
        'ev_w_in': nrm(ks[5], (N_EVEN, D_MODEL, EV_IN), D_MODEL ** -0.5),
        'ev_mu': jax.random.uniform(ks[6], (N_EVEN, RW_IN), f32, 0.0, 1.0),
        'pool_w': nrm(ks[7], (N_EVEN, N_POOL_GROUPS, POOL_GROUP_DIM, POOL_GROUP_DIM), POOL_GROUP_DIM ** -0.5),
        'pool_scale': 1.0 + nrm(ks[8], (N_EVEN, POOL_DIM), 0.1),
        'rw_w0': jax.random.uniform(ks[9], (N_EVEN, RW_DIM), f32, -6.5, -1.0),
        'rw_w2': nrm(ks[10], (N_EVEN, RW_DECAY_LORA, RW_DIM), RW_DECAY_LORA ** -0.5),
        'rw_a0': nrm(ks[11], (N_EVEN, RW_DIM), 0.1),
        'rw_a2': nrm(ks[12], (N_EVEN, RW_A_LORA, RW_DIM), RW_A_LORA ** -0.5),
        'rw_g2': nrm(ks[13], (N_EVEN, RW_GATE_LORA, RW_DIM), RW_GATE_LORA ** -0.5),
        'rw_kk': 0.85 + nrm(ks[14], (N_EVEN, RW_DIM), 0.02),
        'rw_ka': 1.0 + nrm(ks[15], (N_EVEN, RW_DIM), 0.02),
        'rw_rk': -0.04 + nrm(ks[16], (N_EVEN, RW_HEADS, RW_HEAD), 0.02),
        'rw_ln_g': 1.0 + nrm(ks[17], (N_EVEN, RW_DIM), 0.02),
        'rw_ln_b': nrm(ks[18], (N_EVEN, RW_DIM), 0.02),
        'ev_w_out': nrm(ks[19], (N_EVEN, POOL_DIM + RW_DIM, D_MODEL), (POOL_DIM + RW_DIM) ** -0.5),
        'od_w_in': nrm(ks[20], (N_ODD, D_MODEL, OD_IN), D_MODEL ** -0.5),
        'cv_dw': nrm(ks[21], (N_ODD, CV_WIDTH, CV_DIM), CV_WIDTH ** -0.5),
        'cv_db': nrm(ks[22], (N_ODD, CV_DIM), 0.02),
        'cv_ln_g': 1.0 + nrm(ks[23], (N_ODD, CV_DIM), 0.02),
        'cv_ln_b': nrm(ks[24], (N_ODD, CV_DIM), 0.02),
        'sg_ln_g': 1.0 + nrm(ks[25], (N_ODD, SG_DIM), 0.02),
        'sg_ln_b': nrm(ks[26], (N_ODD, SG_DIM), 0.02),
        'sg_ws': nrm(ks[27], (N_ODD, SG_GROUPS, SG_CHUNK, SG_CHUNK), SG_CHUNK ** -0.5),
        'sg_b': 1.0 + nrm(ks[28], (N_ODD, SG_GROUPS, SG_CHUNK), 0.02),
        'od_w_out': nrm(ks[29], (N_ODD, CV_DIM + SG_DIM, D_MODEL), (CV_DIM + SG_DIM) ** -0.5),
    }


def reference(x, norm_g, ffn_w1, ffn_w3, ffn_w2, ev_w_in, ev_mu, pool_w, pool_scale, rw_w0, rw_w2,
              rw_a0, rw_a2, rw_g2, rw_kk, rw_ka, rw_rk, rw_ln_g, rw_ln_b, ev_w_out, od_w_in, cv_dw,
              cv_db, cv_ln_g, cv_ln_b, sg_ln_g, sg_ln_b, sg_ws, sg_b, od_w_out):
    h = x
    for layer in range(DEPTH):
        g = norm_g[layer]
        f = swiglu(rms_norm(h, g[0]), ffn_w1[layer, 0], ffn_w3[layer, 0], ffn_w2[layer, 0])
        h = h + 0.5 * rms_norm(f, g[1]).astype(h.dtype)
        z = rms_norm(h, g[2])
        if layer % 2 == 0:
            e = layer // 2
            p = z @ ev_w_in[e]
            m = jnp.concatenate([
                multiscale_pool(p[..., :POOL_DIM], pool_w[e], pool_scale[e]),
                rwkv7_mix(p[..., POOL_DIM:], ev_mu[e], rw_w0[e], rw_w2[e], rw_a0[e], rw_a2[e], rw_g2[e],
                          rw_kk[e], rw_ka[e], rw_rk[e], rw_ln_g[e], rw_ln_b[e]),
            ], axis=-1)
            m = m @ ev_w_out[e]
        else:
            o = layer // 2
            p = z @ od_w_in[o]
            m = jnp.concatenate([
                conformer_conv(p[..., :2 * CV_DIM], cv_dw[o], cv_db[o], cv_ln_g[o], cv_ln_b[o]),
                chunked_spatial_gate(p[..., 2 * CV_DIM:], sg_ln_g[o], sg_ln_b[o], sg_ws[o], sg_b[o]),
            ], axis=-1)
            m = m @ od_w_out[o]
        h = h + rms_norm(m, g[3]).astype(h.dtype)
        f = swiglu(rms_norm(h, g[4]), ffn_w1[layer, 1], ffn_w3[layer, 1], ffn_w2[layer, 1])
        h = h + 0.5 * rms_norm(f, g[5]).astype(h.dtype)
    return h
```

```python
import functools

import jax
import jax.numpy as jnp
from jax import lax
from jax.experimental import pallas as pl
from jax.experimental.pallas import tpu as pltpu

F32 = jnp.float32
BF16 = jnp.bfloat16
HI = lax.Precision.HIGHEST

D_MODEL = 2048
D_FF = 5632
NORM_EPS = 1e-6
LN_EPS = 1e-5
POOL_DIM = 1024
POOL_WINDOWS = (2, 4, 8, 16)
POOL_GROUP_DIM = 256
RW_DIM = 1024
RW_HEAD = 64
RW_DECAY_LORA = 64
RW_A_LORA = 64
RW_GATE_LORA = 160
RW_GN_EPS = 64e-5
EV_IN = POOL_DIM + 3 * RW_DIM + RW_DECAY_LORA + RW_A_LORA + RW_GATE_LORA
LORA_PAD = 512
EV_IN_PAD = POOL_DIM + 3 * RW_DIM + LORA_PAD
CV_DIM = 1024
CV_WIDTH = 31
CV_HALO = 32
SG_DIM = 1024
SG_CHUNK = 128
SG_GROUPS = 8
OD_IN = 2 * CV_DIM + 2 * SG_DIM

SCAN_CHUNK = 64
SCAN_LANES = 256
SCAN_HEADS = SCAN_LANES // RW_HEAD

VMEM_LIMIT = 56 * 1024 * 1024


def _cparams(sem):
    return pltpu.CompilerParams(dimension_semantics=sem, vmem_limit_bytes=VMEM_LIMIT)


def _rms(x, g):
    return x * lax.rsqrt(jnp.mean(x * x, axis=-1, keepdims=True) + NORM_EPS) * g


def _layer_norm(x, g, b):
    mu = jnp.mean(x, axis=-1, keepdims=True)
    xc = x - mu
    var = jnp.mean(xc * xc, axis=-1, keepdims=True)
    return xc * lax.rsqrt(var + LN_EPS) * g + b


def _sigmoid(x):
    return jax.nn.sigmoid(x)


def _gelu(x):
    return 0.5 * x * (1.0 + lax.erf(x * (2.0 ** -0.5)))


def _head_mask(n):
    r = lax.broadcasted_iota(jnp.int32, (n, n), 0) // RW_HEAD
    c = lax.broadcasted_iota(jnp.int32, (n, n), 1) // RW_HEAD
    return r == c


def _head_sum(x):
    ones_bd = _head_mask(SCAN_LANES).astype(F32)
    parts = [
        jnp.dot(x[:, i:i + SCAN_LANES], ones_bd, precision=HI, preferred_element_type=F32)
        for i in range(0, x.shape[1], SCAN_LANES)
    ]
    return jnp.concatenate(parts, axis=1)


def _ffn_kernel(h_ref, gpre_ref, gpost_ref, w1_ref, w3_ref, w2_ref, o_ref, xn_ref, acc_ref):
    j = pl.program_id(1)

    @pl.when(j == 0)
    def _():
        xn_ref[...] = _rms(h_ref[...], gpre_ref[...]).astype(BF16)
        acc_ref[...] = jnp.zeros_like(acc_ref)

    x = xn_ref[...]
    a = jnp.dot(x, w1_ref[...], preferred_element_type=F32)
    b = jnp.dot(x, w3_ref[...], preferred_element_type=F32)
    hid = (a * _sigmoid(a) * b).astype(BF16)
    acc_ref[...] += jnp.dot(hid, w2_ref[...], preferred_element_type=F32)

    @pl.when(j == pl.num_programs(1) - 1)
    def _():
        o_ref[...] = h_ref[...] + 0.5 * _rms(acc_ref[...], gpost_ref[...])


def _ffn(h, gpre, gpost, w1, w3, w2, layer, idx, tm=512, tf=512):
    m = h.shape[0]
    return pl.pallas_call(
        _ffn_kernel,
        grid=(m // tm, D_FF // tf),
        in_specs=[
            pl.BlockSpec((tm, D_MODEL), lambda i, j: (i, 0)),
            pl.BlockSpec((1, D_MODEL), lambda i, j: (0, 0)),
            pl.BlockSpec((1, D_MODEL), lambda i, j: (0, 0)),
            pl.BlockSpec((None, None, D_MODEL, tf), lambda i, j: (layer, idx, 0, j)),
            pl.BlockSpec((None, None, D_MODEL, tf), lambda i, j: (layer, idx, 0, j)),
            pl.BlockSpec((None, None, tf, D_MODEL), lambda i, j: (layer, idx, j, 0)),
        ],
        out_specs=pl.BlockSpec((tm, D_MODEL), lambda i, j: (i, 0)),
        out_shape=jax.ShapeDtypeStruct((m, D_MODEL), F32),
        scratch_shapes=[pltpu.VMEM((tm, D_MODEL), BF16), pltpu.VMEM((tm, D_MODEL), F32)],
        compiler_params=_cparams(("parallel", "arbitrary")),
        name="ffn",
    )(h, gpre, gpost, w1, w3, w2)


def _inproj_kernel(h_ref, g_ref, w_ref, o_ref, xn_ref):
    @pl.when(pl.program_id(1) == 0)
    def _():
        xn_ref[...] = _rms(h_ref[...], g_ref[...]).astype(BF16)

    o_ref[...] = jnp.dot(xn_ref[...], w_ref[...], preferred_element_type=F32)


def _inproj(h, g, w, tm=1024, tn=512):
    m = h.shape[0]
    n = w.shape[1]
    return pl.pallas_call(
        _inproj_kernel,
        grid=(m // tm, n // tn),
        in_specs=[
            pl.BlockSpec((tm, D_MODEL), lambda i, j: (i, 0)),
            pl.BlockSpec((1, D_MODEL), lambda i, j: (0, 0)),
            pl.BlockSpec((D_MODEL, tn), lambda i, j: (0, j)),
        ],
        out_specs=pl.BlockSpec((tm, tn), lambda i, j: (i, j)),
        out_shape=jax.ShapeDtypeStruct((m, n), F32),
        scratch_shapes=[pltpu.VMEM((tm, D_MODEL), BF16)],
        compiler_params=_cparams(("parallel", "arbitrary")),
        name="inproj",
    )(h, g, w)


def _outproj_kernel(x1_ref, x2_ref, w_ref, h_ref, g_ref, o_ref):
    n1 = x1_ref.shape[1]
    m = jnp.dot(x1_ref[...], w_ref[0:n1, :], preferred_element_type=F32)
    m = m + jnp.dot(x2_ref[...], w_ref[n1:, :], preferred_element_type=F32)
    o_ref[...] = h_ref[...] + _rms(m, g_ref[...])


def _outproj(x1, x2, w, h, g, tm=512):
    m = h.shape[0]
    n1, n2 = x1.shape[1], x2.shape[1]
    return pl.pallas_call(
        _outproj_kernel,
        grid=(m // tm,),
        in_specs=[
            pl.BlockSpec((tm, n1), lambda i: (i, 0)),
            pl.BlockSpec((tm, n2), lambda i: (i, 0)),
            pl.BlockSpec((n1 + n2, D_MODEL), lambda i: (0, 0)),
            pl.BlockSpec((tm, D_MODEL), lambda i: (i, 0)),
            pl.BlockSpec((1, D_MODEL), lambda i: (0, 0)),
        ],
        out_specs=pl.BlockSpec((tm, D_MODEL), lambda i: (i, 0)),
        out_shape=jax.ShapeDtypeStruct((m, D_MODEL), F32),
        compiler_params=_cparams(("parallel",)),
        name="outproj",
    )(x1, x2, w, h, g)


def _shift_rows(x, k):
    row = lax.broadcasted_iota(jnp.int32, x.shape, 0)
    return jnp.where(row >= k, pltpu.roll(x, k, axis=0), 0.0)


def _pool_kernel(p_ref, w_ref, s_ref, o_ref):
    gi = pl.program_id(1)
    x = p_ref[0]
    s2 = x + _shift_rows(x, 1)
    s4 = s2 + _shift_rows(s2, 2)
    s8 = s4 + _shift_rows(s4, 4)
    s16 = s8 + _shift_rows(s8, 8)
    wsum = jnp.where(gi == 0, s2, jnp.where(gi == 1, s4, jnp.where(gi == 2, s8, s16)))
    win = jnp.left_shift(2, gi)
    t = lax.broadcasted_iota(jnp.int32, (x.shape[0], 1), 0)
    cnt = jnp.minimum(t + 1, win).astype(F32)
    d = (wsum / cnt - x).astype(BF16)
    o_ref[0] = (jnp.dot(d, w_ref[0], preferred_element_type=F32) * s_ref[...]).astype(BF16)


def _pool(p3, pool_w, pool_scale):
    b, t, _ = p3.shape
    n_grp = len(POOL_WINDOWS)
    return pl.pallas_call(
        _pool_kernel,
        grid=(b, n_grp),
        in_specs=[
            pl.BlockSpec((1, t, POOL_GROUP_DIM), lambda i, g: (i, 0, g)),
            pl.BlockSpec((1, POOL_GROUP_DIM, POOL_GROUP_DIM), lambda i, g: (g, 0, 0)),
            pl.BlockSpec((1, POOL_GROUP_DIM), lambda i, g: (0, g)),
        ],
        out_specs=pl.BlockSpec((1, t, POOL_GROUP_DIM), lambda i, g: (i, 0, g)),
        out_shape=jax.ShapeDtypeStruct((b, t, POOL_DIM), BF16),
        compiler_params=_cparams(("parallel", "parallel")),
        name="pool",
    )(p3, pool_w, pool_scale)


def _rwprep_kernel(r_ref, k_ref, v_ref, l_ref, rp_ref, kp_ref, vp_ref, lp_ref,
                   mur_ref, muk_ref, muv_ref, mul_ref, w0_ref, w2_ref, a0_ref, a2_ref, g2_ref,
                   kk_ref, ka_ref, rk_ref,
                   ro_ref, lwo_ref, ko_ref, vo_ref, ao_ref, bo_ref, go_ref, bonus_ref):
    ti = pl.program_id(1)

    def token_shift(cur_ref, prev_ref, mu_ref):
        x = cur_ref[0]
        prev_row = jnp.where(ti > 0, prev_ref[0, 7:8, :], 0.0)
        row = lax.broadcasted_iota(jnp.int32, x.shape, 0)
        xs = jnp.where(row == 0, prev_row, pltpu.roll(x, 1, axis=0))
        return x + (xs - x) * mu_ref[...]

    r = token_shift(r_ref, rp_ref, mur_ref)
    k = token_shift(k_ref, kp_ref, muk_ref)
    v = token_shift(v_ref, vp_ref, muv_ref)
    lo = token_shift(l_ref, lp_ref, mul_ref)

    lo_wa = lo[:, 0:128]
    xw = w0_ref[...] + jnp.dot(jnp.tanh(lo_wa), w2_ref[...], precision=HI, preferred_element_type=F32)
    nx = -xw
    softplus = jnp.maximum(nx, 0.0) + jnp.log1p(jnp.exp(-jnp.abs(nx)))
    w_log = -softplus - 0.5
    lwo_ref[0] = -jnp.exp(w_log)
    a = _sigmoid(a0_ref[...] + jnp.dot(lo_wa, a2_ref[...], precision=HI, preferred_element_type=F32))
    go_ref[0] = jnp.dot(_sigmoid(lo[:, 128:384]), g2_ref[...], precision=HI, preferred_element_type=F32)

    kk = k * kk_ref[...]
    kk = kk / jnp.maximum(jnp.sqrt(_head_sum(kk * kk)), 1e-12)
    k2 = k * (1.0 + (a - 1.0) * ka_ref[...])
    ro_ref[0] = r
    ko_ref[0] = k2
    vo_ref[0] = v
    ao_ref[0] = -kk
    bo_ref[0] = kk * a
    bonus_ref[0] = _head_sum(r * k2 * rk_ref[...]) * v


def _rwprep(p3, mu, w0, w2p, a0, a2p, g2p, kkp, kap, rkp, tt=256):
    b, t, _ = p3.shape
    nt = t // tt
    blk8 = tt // 8
    col = lambda c: (lambda i, j: (i, j, c))
    prev = lambda c: (lambda i, j: (i, jnp.maximum(j * blk8 - 1, 0), c))
    vec = lambda n: pl.BlockSpec((1, n), lambda i, j: (0, 0))
    lora_col = (POOL_DIM + 3 * RW_DIM) // LORA_PAD
    in_specs = [
        pl.BlockSpec((1, tt, RW_DIM), col(1)),
        pl.BlockSpec((1, tt, RW_DIM), col(2)),
        pl.BlockSpec((1, tt, RW_DIM), col(3)),
        pl.BlockSpec((1, tt, LORA_PAD), col(lora_col)),
        pl.BlockSpec((1, 8, RW_DIM), prev(1)),
        pl.BlockSpec((1, 8, RW_DIM), prev(2)),
        pl.BlockSpec((1, 8, RW_DIM), prev(3)),
        pl.BlockSpec((1, 8, LORA_PAD), prev(lora_col)),
        vec(RW_DIM), vec(RW_DIM), vec(RW_DIM), vec(LORA_PAD),
        vec(RW_DIM),
        pl.BlockSpec((128, RW_DIM), lambda i, j: (0, 0)),
        vec(RW_DIM),
        pl.BlockSpec((128, RW_DIM), lambda i, j: (0, 0)),
        pl.BlockSpec((256, RW_DIM), lambda i, j: (0, 0)),
        vec(RW_DIM), vec(RW_DIM), vec(RW_DIM),
    ]
    out_spec = pl.BlockSpec((1, tt, RW_DIM), lambda i, j: (i, j, 0))
    out_sds = jax.ShapeDtypeStruct((b, t, RW_DIM), F32)
    return pl.pallas_call(
        _rwprep_kernel,
        grid=(b, nt),
        in_specs=in_specs,
        out_specs=[out_spec] * 8,
        out_shape=[out_sds] * 8,
        compiler_params=_cparams(("parallel", "parallel")),
        name="rwprep",
    )(p3, p3, p3, p3, p3, p3, p3, p3, mu[0], mu[1], mu[2], mu[3], w0, w2p, a0, a2p, g2p, kkp, kap, rkp)


def _scan_kernel(r_ref, lw_ref, k_ref, v_ref, a_ref, b_ref, y_ref, h_ref):
    c_len, lanes = SCAN_CHUNK, SCAN_LANES

    @pl.when(pl.program_id(2) == 0)
    def _():
        h_ref[...] = jnp.zeros_like(h_ref)

    r, lw, k, v, a, b = r_ref[0], lw_ref[0], k_ref[0], v_ref[0], a_ref[0], b_ref[0]

    bmask = _head_mask(lanes)
    row = lax.broadcasted_iota(jnp.int32, (c_len, lanes), 0)
    src = lax.broadcasted_iota(jnp.int32, (c_len, lanes), 1) % c_len
    strict, incl = src < row, src <= row
    lrow = lax.broadcasted_iota(jnp.int32, (c_len, c_len), 0)
    lcol = lax.broadcasted_iota(jnp.int32, (c_len, c_len), 1)
    ltri = (lcol <= lrow).astype(F32)

    def dot(x, y):
        return jnp.dot(x, y, precision=HI, preferred_element_type=F32)

    def stack(x):
        return jnp.where(bmask, jnp.concatenate([x] * SCAN_HEADS, axis=0), 0.0)

    def apply(p, x):
        return dot(p, stack(x))

    def scores(x, y):
        return lax.dot_general(x, stack(y), (((1,), (1,)), ((), ())), precision=HI, preferred_element_type=F32)

    cs = dot(ltri, lw)
    tot = cs[c_len - 1:c_len, :]
    e_pos, e_neg, e_rem = jnp.exp(cs), jnp.exp(-cs), jnp.exp(tot - cs)
    a_t = a * jnp.exp(cs - lw)
    r_t = r * e_pos
    b_t, k_t = b * e_neg, k * e_neg
    b_h, k_h = b * e_rem, k * e_rem

    ar = jnp.concatenate([a_t, r_t], axis=0)
    sb, sk = scores(ar, b_t), scores(ar, k_t)
    p_ab = jnp.where(strict, sb[:c_len], 0.0)
    p_ak = jnp.where(strict, sk[:c_len], 0.0)
    p_rb = jnp.where(incl, sb[c_len:], 0.0)
    p_rk = jnp.where(incl, sk[c_len:], 0.0)

    t_inv = jnp.where(src == row, 1.0, 0.0) + p_ab
    q = p_ab
    for _ in range(5):
        q = apply(q, q)
        t_inv = t_inv + apply(q, t_inv)

    a2 = apply(t_inv, a_t)
    v2 = apply(t_inv, apply(p_ak, v))
    r2 = r_t + apply(p_rb, a2)
    y2 = apply(p_rb, v2) + apply(p_rk, v)
    eye = (lax.broadcasted_iota(jnp.int32, (lanes, lanes), 0)
           == lax.broadcasted_iota(jnp.int32, (lanes, lanes), 1))
    m_c = jnp.where(bmask, dot(b_h.T, a2), 0.0) + jnp.where(eye, jnp.exp(tot), 0.0)
    n_c = jnp.where(bmask, dot(b_h.T, v2) + dot(k_h.T, v), 0.0)

    h0 = h_ref[...]
    y_ref[0] = dot(r2, h0) + y2
    h_ref[...] = dot(m_c, h0) + n_c


def _scan(r, lw, k, v, a, b):
    bsz, t, _ = r.shape
    spec = pl.BlockSpec((1, SCAN_CHUNK, SCAN_LANES), lambda i, g, c: (i, c, g))
    return pl.pallas_call(
        _scan_kernel,
        grid=(bsz, RW_DIM // SCAN_LANES, t // SCAN_CHUNK),
        in_specs=[spec] * 6,
        out_specs=spec,
        out_shape=jax.ShapeDtypeStruct((bsz, t, RW_DIM), F32),
        scratch_shapes=[pltpu.VMEM((SCAN_LANES, SCAN_LANES), F32)],
        compiler_params=_cparams(("parallel", "parallel", "arbitrary")),
        name="wkv7_scan",
    )(r, lw, k, v, a, b)


def _rwpost_kernel(y_ref, bonus_ref, g_ref, lng_ref, lnb_ref, o_ref):
    y = y_ref[...]
    ym = _head_sum(y) * (1.0 / RW_HEAD)
    yc = y - ym
    yv = _head_sum(yc * yc) * (1.0 / RW_HEAD)
    out = yc * lax.rsqrt(yv + RW_GN_EPS) * lng_ref[...] + lnb_ref[...] + bonus_ref[...]
    o_ref[...] = (out * g_ref[...]).astype(BF16)


def _rwpost(y, bonus, g, ln_g, ln_b, tt=512):
    m = y.shape[0]
    spec = pl.BlockSpec((tt, RW_DIM), lambda i: (i, 0))
    vec = pl.BlockSpec((1, RW_DIM), lambda i: (0, 0))
    return pl.pallas_call(
        _rwpost_kernel,
        grid=(m // tt,),
        in_specs=[spec, spec, spec, vec, vec],
        out_specs=spec,
        out_shape=jax.ShapeDtypeStruct((m, RW_DIM), BF16),
        compiler_params=_cparams(("parallel",)),
        name="rwpost",
    )(y, bonus, g, ln_g, ln_b)


def _conv_kernel(a_ref, gate_ref, dw_ref, db_ref, lng_ref, lnb_ref, o_ref, buf_ref, y_ref):
    tt = a_ref.shape[1]
    rb, cb = 64, 128

    @pl.when(pl.program_id(1) == 0)
    def _():
        buf_ref[0:CV_HALO, :] = jnp.zeros((CV_HALO, CV_DIM), F32)

    buf_ref[CV_HALO:CV_HALO + tt, :] = a_ref[0] * _sigmoid(gate_ref[0])
    first = CV_HALO - (CV_WIDTH - 1)
    for r0 in range(0, tt, rb):
        for c0 in range(0, CV_DIM, cb):
            acc = jnp.broadcast_to(db_ref[:, c0:c0 + cb], (rb, cb))
            for j in range(CV_WIDTH):
                acc = acc + dw_ref[j:j + 1, c0:c0 + cb] * buf_ref[first + j + r0:first + j + r0 + rb, c0:c0 + cb]
            y_ref[r0:r0 + rb, c0:c0 + cb] = acc
    buf_ref[0:CV_HALO, :] = buf_ref[tt:tt + CV_HALO, :]
    z = _layer_norm(y_ref[...], lng_ref[...], lnb_ref[...])
    o_ref[0] = (z * _sigmoid(z)).astype(BF16)


def _conv(p3, dw, db, ln_g, ln_b, tt=256):
    b, t, _ = p3.shape
    vec = pl.BlockSpec((1, CV_DIM), lambda i, j: (0, 0))
    return pl.pallas_call(
        _conv_kernel,
        grid=(b, t // tt),
        in_specs=[
            pl.BlockSpec((1, tt, CV_DIM), lambda i, j: (i, j, 0)),
            pl.BlockSpec((1, tt, CV_DIM), lambda i, j: (i, j, 1)),
            pl.BlockSpec((CV_HALO, CV_DIM), lambda i, j: (0, 0)),
            vec, vec, vec,
        ],
        out_specs=pl.BlockSpec((1, tt, CV_DIM), lambda i, j: (i, j, 0)),
        out_shape=jax.ShapeDtypeStruct((b, t, CV_DIM), BF16),
        scratch_shapes=[pltpu.VMEM((CV_HALO + tt, CV_DIM), F32), pltpu.VMEM((tt, CV_DIM), F32)],
        compiler_params=_cparams(("parallel", "arbitrary")),
        name="conv",
    )(p3, p3, dw, db, ln_g, ln_b)


def _gmlp_kernel(u_ref, v_ref, lng_ref, lnb_ref, ws_ref, bs_ref, o_ref):
    tt = u_ref.shape[1]
    n_chunks = tt // SG_CHUNK
    grp = SG_DIM // SG_GROUPS
    u = _gelu(u_ref[0])
    v = _gelu(v_ref[0])
    vb = _layer_norm(v, lng_ref[...], lnb_ref[...]).astype(BF16)
    tri = (lax.broadcasted_iota(jnp.int32, (SG_CHUNK, SG_CHUNK), 1)
           <= lax.broadcasted_iota(jnp.int32, (SG_CHUNK, SG_CHUNK), 0))
    for g in range(SG_GROUPS):
        lanes = slice(g * grp, (g + 1) * grp)
        wg = jnp.where(tri, ws_ref[g], 0.0).astype(BF16)
        vg = jnp.concatenate([vb[c * SG_CHUNK:(c + 1) * SG_CHUNK, lanes] for c in range(n_chunks)], axis=1)
        sg = jnp.dot(wg, vg, preferred_element_type=F32)
        bias = bs_ref[:, g:g + 1]
        for c in range(n_chunks):
            rows = slice(c * SG_CHUNK, (c + 1) * SG_CHUNK)
            s = sg[:, c * grp:(c + 1) * grp] + bias
            o_ref[0, rows, lanes] = (u[rows, lanes] * s).astype(BF16)


def _gmlp(p3, ln_g, ln_b, ws, bs_t, tt=512):
    b, t, _ = p3.shape
    vec = pl.BlockSpec((1, SG_DIM), lambda i, j: (0, 0))
    return pl.pallas_call(
        _gmlp_kernel,
        grid=(b, t // tt),
        in_specs=[
            pl.BlockSpec((1, tt, SG_DIM), lambda i, j: (i, j, 2)),
            pl.BlockSpec((1, tt, SG_DIM), lambda i, j: (i, j, 3)),
            vec, vec,
            pl.BlockSpec((SG_GROUPS, SG_CHUNK, SG_CHUNK), lambda i, j: (0, 0, 0)),
            pl.BlockSpec((SG_CHUNK, SG_GROUPS), lambda i, j: (0, 0)),
        ],
        out_specs=pl.BlockSpec((1, tt, SG_DIM), lambda i, j: (i, j, 0)),
        out_shape=jax.ShapeDtypeStruct((b, t, SG_DIM), BF16),
        compiler_params=_cparams(("parallel", "parallel")),
        name="gmlp",
    )(p3, p3, ln_g, ln_b, ws, bs_t)


def _pad_rows(w, start, rows):
    return jnp.zeros((rows, w.shape[1]), w.dtype).at[start:start + w.shape[0]].set(w)


def kernel(x, norm_g, ffn_w1, ffn_w3, ffn_w2, ev_w_in, ev_mu, pool_w, pool_scale, rw_w0, rw_w2, rw_a0, rw_a2,
           rw_g2, rw_kk, rw_ka, rw_rk, rw_ln_g, rw_ln_b, ev_w_out, od_w_in, cv_dw, cv_db, cv_ln_g, cv_ln_b,
           sg_ln_g, sg_ln_b, sg_ws, sg_b, od_w_out):
    bsz, t, d = x.shape
    m = bsz * t
    depth = norm_g.shape[0]
    row = lambda vct: vct.reshape(1, -1)

    w1, w3, w2 = ffn_w1.astype(BF16), ffn_w3.astype(BF16), ffn_w2.astype(BF16)
    h = x.reshape(m, d)
    for layer in range(depth):
        g = norm_g[layer]
        h = _ffn(h, row(g[0]), row(g[1]), w1, w3, w2, layer, 0)
        if layer % 2 == 0:
            e = layer // 2
            w_in = jnp.pad(ev_w_in[e], ((0, 0), (0, EV_IN_PAD - EV_IN))).astype(BF16)
            p3 = _inproj(h, row(g[2]), w_in).reshape(bsz, t, EV_IN_PAD)
            m1 = _pool(p3, pool_w[e].astype(BF16), row(pool_scale[e])).reshape(m, POOL_DIM)
            mu = ev_mu[e]
            mus = (row(mu[0:RW_DIM]), row(mu[RW_DIM:2 * RW_DIM]), row(mu[2 * RW_DIM:3 * RW_DIM]),
                   row(jnp.pad(mu[3 * RW_DIM:], (0, LORA_PAD - (EV_IN - POOL_DIM - 3 * RW_DIM)))))
            r, lw, k, v, a, b, gate, bonus = _rwprep(
                p3, mus, row(rw_w0[e]), _pad_rows(rw_w2[e], 0, 128), row(rw_a0[e]),
                _pad_rows(rw_a2[e], RW_DECAY_LORA, 128), _pad_rows(rw_g2[e], 0, 256),
                row(rw_kk[e]), row(rw_ka[e]), row(rw_rk[e]))
            y = _scan(r, lw, k, v, a, b)
            m2 = _rwpost(y.reshape(m, RW_DIM), bonus.reshape(m, RW_DIM), gate.reshape(m, RW_DIM),
                         row(rw_ln_g[e]), row(rw_ln_b[e]))
            h = _outproj(m1, m2, ev_w_out[e].astype(BF16), h, row(g[3]))
        else:
            o = layer // 2
            p3 = _inproj(h, row(g[2]), od_w_in[o].astype(BF16)).reshape(bsz, t, OD_IN)
            dw = jnp.pad(cv_dw[o], ((0, CV_HALO - CV_WIDTH), (0, 0)))
            m1 = _conv(p3, dw, row(cv_db[o]), row(cv_ln_g[o]), row(cv_ln_b[o])).reshape(m, CV_DIM)
            m2 = _gmlp(p3, row(sg_ln_g[o]), row(sg_ln_b[o]), sg_ws[o], sg_b[o].T).reshape(m, SG_DIM)
            h = _outproj(m1, m2, od_w_out[o].astype(BF16), h, row(g[3]))
        h = _ffn(h, row(g[4]), row(g[5]), w1, w3, w2, layer, 1)
    return h.reshape(bsz, t, d)
```

```python
import functools

import jax
import jax.numpy as jnp
from jax import lax
from jax.experimental import pallas as pl
from jax.experimental.pallas import tpu as pltpu

F32 = jnp.float32
BF16 = jnp.bfloat16

D_MODEL = 2048
D_FF = 5632
NORM_EPS = 1e-6
LN_EPS = 1e-5
POOL_DIM = 1024
POOL_WINDOWS = (2, 4, 8, 16)
POOL_GROUP_DIM = 256
RW_DIM = 1024
RW_HEAD = 64
RW_DECAY_LORA = 64
RW_A_LORA = 64
RW_GATE_LORA = 160
RW_GN_EPS = 64e-5
EV_IN = POOL_DIM + 3 * RW_DIM + RW_DECAY_LORA + RW_A_LORA + RW_GATE_LORA
LORA_PAD = 512
EV_IN_PAD = POOL_DIM + 3 * RW_DIM + LORA_PAD
CV_DIM = 1024
CV_WIDTH = 31
CV_HALO = 32
SG_DIM = 1024
SG_CHUNK = 128
SG_GROUPS = 8
OD_IN = 2 * CV_DIM + 2 * SG_DIM

SCAN_CHUNK = 64
SCAN_LANES = 256
SCAN_HEADS = SCAN_LANES // RW_HEAD

VMEM_LIMIT = 56 * 1024 * 1024


def _cparams(sem):
    return pltpu.CompilerParams(dimension_semantics=sem, vmem_limit_bytes=VMEM_LIMIT)


def _rms(x, g):
    return x * lax.rsqrt(jnp.mean(x * x, axis=-1, keepdims=True) + NORM_EPS) * g


def _layer_norm(x, g, b):
    mu = jnp.mean(x, axis=-1, keepdims=True)
    xc = x - mu
    var = jnp.mean(xc * xc, axis=-1, keepdims=True)
    return xc * lax.rsqrt(var + LN_EPS) * g + b


def _sigmoid(x):
    return jax.nn.sigmoid(x)


def _gelu(x):
    return 0.5 * x * (1.0 + lax.erf(x * (2.0 ** -0.5)))


def _head_mask(n):
    r = lax.broadcasted_iota(jnp.int32, (n, n), 0) // RW_HEAD
    c = lax.broadcasted_iota(jnp.int32, (n, n), 1) // RW_HEAD
    return r == c


def _bf16_terms(x, n):
    terms = []
    for _ in range(n - 1):
        t = x.astype(BF16)
        terms.append(t)
        x = x - t.astype(F32)
    terms.append(x.astype(BF16))
    return terms


def _head_sum(x):
    ones_bd = _head_mask(SCAN_LANES).astype(BF16)
    parts = []
    for i in range(0, x.shape[1], SCAN_LANES):
        hi, lo = _bf16_terms(x[:, i:i + SCAN_LANES], 2)
        parts.append(jnp.dot(hi, ones_bd, preferred_element_type=F32)
                     + jnp.dot(lo, ones_bd, preferred_element_type=F32))
    return jnp.concatenate(parts, axis=1)


def _ffn_kernel(h_ref, gpre_ref, gpost_ref, w1_ref, w3_ref, w2_ref, o_ref, xn_ref, acc_ref):
    j = pl.program_id(1)

    @pl.when(j == 0)
    def _():
        xn_ref[...] = _rms(h_ref[...], gpre_ref[...]).astype(BF16)
        acc_ref[...] = jnp.zeros_like(acc_ref)

    x = xn_ref[...]
    a = jnp.dot(x, w1_ref[...], preferred_element_type=F32)
    b = jnp.dot(x, w3_ref[...], preferred_element_type=F32)
    hid = (a * _sigmoid(a) * b).astype(BF16)
    acc_ref[...] += jnp.dot(hid, w2_ref[...], preferred_element_type=F32)

    @pl.when(j == pl.num_programs(1) - 1)
    def _():
        o_ref[...] = h_ref[...] + 0.5 * _rms(acc_ref[...], gpost_ref[...])


def _ffn(h, gpre, gpost, w1, w3, w2, layer, idx, tm=512, tf=512):
    m = h.shape[0]
    return pl.pallas_call(
        _ffn_kernel,
        grid=(m // tm, D_FF // tf),
        in_specs=[
            pl.BlockSpec((tm, D_MODEL), lambda i, j: (i, 0)),
            pl.BlockSpec((1, D_MODEL), lambda i, j: (0, 0)),
            pl.BlockSpec((1, D_MODEL), lambda i, j: (0, 0)),
            pl.BlockSpec((None, None, D_MODEL, tf), lambda i, j: (layer, idx, 0, j)),
            pl.BlockSpec((None, None, D_MODEL, tf), lambda i, j: (layer, idx, 0, j)),
            pl.BlockSpec((None, None, tf, D_MODEL), lambda i, j: (layer, idx, j, 0)),
        ],
        out_specs=pl.BlockSpec((tm, D_MODEL), lambda i, j: (i, 0)),
        out_shape=jax.ShapeDtypeStruct((m, D_MODEL), F32),
        scratch_shapes=[pltpu.VMEM((tm, D_MODEL), BF16), pltpu.VMEM((tm, D_MODEL), F32)],
        compiler_params=_cparams(("parallel", "arbitrary")),
        name="ffn",
    )(h, gpre, gpost, w1, w3, w2)


def _inproj_kernel(h_ref, g_ref, w_ref, o_ref, xn_ref):
    @pl.when(pl.program_id(1) == 0)
    def _():
        xn_ref[...] = _rms(h_ref[...], g_ref[...]).astype(BF16)

    o_ref[...] = jnp.dot(xn_ref[...], w_ref[...], preferred_element_type=F32)


def _inproj(h, g, w, tn, tm=1024):
    m = h.shape[0]
    n = w.shape[1]
    return pl.pallas_call(
        _inproj_kernel,
        grid=(m // tm, n // tn),
        in_specs=[
            pl.BlockSpec((tm, D_MODEL), lambda i, j: (i, 0)),
            pl.BlockSpec((1, D_MODEL), lambda i, j: (0, 0)),
            pl.BlockSpec((D_MODEL, tn), lambda i, j: (0, j)),
        ],
        out_specs=pl.BlockSpec((tm, tn), lambda i, j: (i, j)),
        out_shape=jax.ShapeDtypeStruct((m, n), F32),
        scratch_shapes=[pltpu.VMEM((tm, D_MODEL), BF16)],
        compiler_params=_cparams(("parallel", "arbitrary")),
        name="inproj",
    )(h, g, w)


def _outproj_kernel(x1_ref, x2_ref, w_ref, h_ref, g_ref, o_ref):
    n1 = x1_ref.shape[1]
    m = jnp.dot(x1_ref[...], w_ref[0:n1, :], preferred_element_type=F32)
    m = m + jnp.dot(x2_ref[...], w_ref[n1:, :], preferred_element_type=F32)
    o_ref[...] = h_ref[...] + _rms(m, g_ref[...])


def _outproj(x1, x2, w, h, g, tm=512):
    m = h.shape[0]
    n1, n2 = x1.shape[1], x2.shape[1]
    return pl.pallas_call(
        _outproj_kernel,
        grid=(m // tm,),
        in_specs=[
            pl.BlockSpec((tm, n1), lambda i: (i, 0)),
            pl.BlockSpec((tm, n2), lambda i: (i, 0)),
            pl.BlockSpec((n1 + n2, D_MODEL), lambda i: (0, 0)),
            pl.BlockSpec((tm, D_MODEL), lambda i: (i, 0)),
            pl.BlockSpec((1, D_MODEL), lambda i: (0, 0)),
        ],
        out_specs=pl.BlockSpec((tm, D_MODEL), lambda i: (i, 0)),
        out_shape=jax.ShapeDtypeStruct((m, D_MODEL), F32),
        compiler_params=_cparams(("parallel",)),
        name="outproj",
    )(x1, x2, w, h, g)


def _shift_rows(x, k):
    row = lax.broadcasted_iota(jnp.int32, x.shape, 0)
    return jnp.where(row >= k, pltpu.roll(x, k, axis=0), 0.0)


def _pool_kernel(p_ref, w_ref, s_ref, o_ref):
    gi = pl.program_id(1)
    x = p_ref[0]
    s2 = x + _shift_rows(x, 1)
    s4 = s2 + _shift_rows(s2, 2)
    s8 = s4 + _shift_rows(s4, 4)
    s16 = s8 + _shift_rows(s8, 8)
    wsum = jnp.where(gi == 0, s2, jnp.where(gi == 1, s4, jnp.where(gi == 2, s8, s16)))
    win = jnp.left_shift(2, gi)
    t = lax.broadcasted_iota(jnp.int32, (x.shape[0], 1), 0)
    cnt = jnp.minimum(t + 1, win).astype(F32)
    d = (wsum / cnt - x).astype(BF16)
    o_ref[0] = (jnp.dot(d, w_ref[0], preferred_element_type=F32) * s_ref[...]).astype(BF16)


def _pool(p3, pool_w, pool_scale):
    b, t, _ = p3.shape
    n_grp = len(POOL_WINDOWS)
    return pl.pallas_call(
        _pool_kernel,
        grid=(b, n_grp),
        in_specs=[
            pl.BlockSpec((1, t, POOL_GROUP_DIM), lambda i, g: (i, 0, g)),
            pl.BlockSpec((1, POOL_GROUP_DIM, POOL_GROUP_DIM), lambda i, g: (g, 0, 0)),
            pl.BlockSpec((1, POOL_GROUP_DIM), lambda i, g: (0, g)),
        ],
        out_specs=pl.BlockSpec((1, t, POOL_GROUP_DIM), lambda i, g: (i, 0, g)),
        out_shape=jax.ShapeDtypeStruct((b, t, POOL_DIM), BF16),
        compiler_params=_cparams(("parallel", "parallel")),
        name="pool",
    )(p3, pool_w, pool_scale)


def _rwprep_kernel(r_ref, k_ref, v_ref, l_ref, rp_ref, kp_ref, vp_ref, lp_ref,
                   mur_ref, muk_ref, muv_ref, mul_ref, w0_ref, w2_ref, a0_ref, a2_ref, g2_ref,
                   kk_ref, ka_ref, rk_ref,
                   ro_ref, cso_ref, ko_ref, vo_ref, ao_ref, bo_ref, go_ref, bonus_ref):
    ti = pl.program_id(1)

    def token_shift(cur_ref, prev_ref, mu_ref):
        x = cur_ref[0]
        prev_row = jnp.where(ti > 0, prev_ref[0, 7:8, :], 0.0)
        row = lax.broadcasted_iota(jnp.int32, x.shape, 0)
        xs = jnp.where(row == 0, prev_row, pltpu.roll(x, 1, axis=0))
        return x + (xs - x) * mu_ref[...]

    r = token_shift(r_ref, rp_ref, mur_ref)
    k = token_shift(k_ref, kp_ref, muk_ref)
    v = token_shift(v_ref, vp_ref, muv_ref)
    lo = token_shift(l_ref, lp_ref, mul_ref)

    lo_wa = lo[:, 0:128]
    xw = w0_ref[...] + jnp.dot(jnp.tanh(lo_wa).astype(BF16), w2_ref[...], preferred_element_type=F32)
    nx = -xw
    softplus = jnp.maximum(nx, 0.0) + jnp.log1p(jnp.exp(-jnp.abs(nx)))
    w_log = -softplus - 0.5
    lw = -jnp.exp(w_log)
    tt = lw.shape[0]
    ci = lax.broadcasted_iota(jnp.int32, (tt, tt), 0)
    cj = lax.broadcasted_iota(jnp.int32, (tt, tt), 1)
    chunk_tri = ((ci // SCAN_CHUNK == cj // SCAN_CHUNK) & (cj <= ci)).astype(BF16)
    cs = sum(jnp.dot(chunk_tri, term, preferred_element_type=F32) for term in _bf16_terms(lw, 3))
    cso_ref[0] = cs
    a = _sigmoid(a0_ref[...] + jnp.dot(lo_wa.astype(BF16), a2_ref[...], preferred_element_type=F32))
    go_ref[0] = jnp.dot(_sigmoid(lo[:, 128:384]).astype(BF16), g2_ref[...], preferred_element_type=F32)

    kk = k * kk_ref[...]
    kk = kk / jnp.maximum(jnp.sqrt(_head_sum(kk * kk)), 1e-12)
    k2 = k * (1.0 + (a - 1.0) * ka_ref[...])
    ro_ref[0] = r
    ko_ref[0] = k2
    vo_ref[0] = v
    ao_ref[0] = -kk * jnp.exp(cs - lw)
    bo_ref[0] = kk * a
    bonus_ref[0] = _head_sum(r * k2 * rk_ref[...]) * v


def _rwprep(p3, mu, w0, w2p, a0, a2p, g2p, kkp, kap, rkp, tt=256):
    b, t, _ = p3.shape
    nt = t // tt
    blk8 = tt // 8
    col = lambda c: (lambda i, j: (i, j, c))
    prev = lambda c: (lambda i, j: (i, jnp.maximum(j * blk8 - 1, 0), c))
    vec = lambda n: pl.BlockSpec((1, n), lambda i, j: (0, 0))
    lora_col = (POOL_DIM + 3 * RW_DIM) // LORA_PAD
    in_specs = [
        pl.BlockSpec((1, tt, RW_DIM), col(1)),
        pl.BlockSpec((1, tt, RW_DIM), col(2)),
        pl.BlockSpec((1, tt, RW_DIM), col(3)),
        pl.BlockSpec((1, tt, LORA_PAD), col(lora_col)),
        pl.BlockSpec((1, 8, RW_DIM), prev(1)),
        pl.BlockSpec((1, 8, RW_DIM), prev(2)),
        pl.BlockSpec((1, 8, RW_DIM), prev(3)),
        pl.BlockSpec((1, 8, LORA_PAD), prev(lora_col)),
        vec(RW_DIM), vec(RW_DIM), vec(RW_DIM), vec(LORA_PAD),
        vec(RW_DIM),
        pl.BlockSpec((128, RW_DIM), lambda i, j: (0, 0)),
        vec(RW_DIM),
        pl.BlockSpec((128, RW_DIM), lambda i, j: (0, 0)),
        pl.BlockSpec((256, RW_DIM), lambda i, j: (0, 0)),
        vec(RW_DIM), vec(RW_DIM), vec(RW_DIM),
    ]
    out_spec = pl.BlockSpec((1, tt, RW_DIM), lambda i, j: (i, j, 0))
    out_sds = jax.ShapeDtypeStruct((b, t, RW_DIM), F32)
    return pl.pallas_call(
        _rwprep_kernel,
        grid=(b, nt),
        in_specs=in_specs,
        out_specs=[out_spec] * 8,
        out_shape=[out_sds] * 8,
        compiler_params=_cparams(("parallel", "parallel")),
        name="rwprep",
    )(p3, p3, p3, p3, p3, p3, p3, p3, mu[0], mu[1], mu[2], mu[3], w0, w2p, a0, a2p, g2p, kkp, kap, rkp)


def _scan_kernel(r_ref, cs_ref, k_ref, v_ref, a_ref, b_ref, y_ref, h_ref):
    c_len, lanes = SCAN_CHUNK, SCAN_LANES

    @pl.when(pl.program_id(1) == 0)
    def _():
        h_ref[...] = jnp.zeros_like(h_ref)

    bmask_f = _head_mask(lanes)
    bmask = bmask_f.astype(BF16)
    bmask2 = jnp.concatenate([bmask, bmask], axis=1)
    row = lax.broadcasted_iota(jnp.int32, (c_len, lanes), 0)
    src = lax.broadcasted_iota(jnp.int32, (c_len, lanes), 1) % c_len
    strict, incl = src < row, src <= row
    ident = jnp.where(src == row, 1.0, 0.0)
    same = lambda n: (row // n) == (src // n)
    eye = (lax.broadcasted_iota(jnp.int32, (lanes, lanes), 0)
           == lax.broadcasted_iota(jnp.int32, (lanes, lanes), 1))

    def mm(x, y):
        return jnp.dot(x.astype(BF16), y.astype(BF16), preferred_element_type=F32)

    def stack(x):
        xb = jnp.concatenate([x.astype(BF16)] * SCAN_HEADS, axis=0)
        return xb * (bmask if x.shape[1] == lanes else bmask2)

    def apply(p, x):
        return jnp.dot(p.astype(BF16), stack(x), preferred_element_type=F32)

    def scores(x, y):
        return lax.dot_general(x.astype(BF16), stack(y), (((1,), (1,)), ((), ())), preferred_element_type=F32)

    def chain(g):
        sl = slice(g * lanes, (g + 1) * lanes)
        r, cs, k, v, a_t, b = (ref[0, :, sl] for ref in (r_ref, cs_ref, k_ref, v_ref, a_ref, b_ref))
        tot = cs[c_len - 1:c_len, :]
        e_neg, e_rem = jnp.exp(-cs), jnp.exp(tot - cs)
        r_t = r * jnp.exp(cs)
        b_t, k_t = b * e_neg, k * e_neg
        b_h, k_h = b * e_rem, k * e_rem

        ar = jnp.concatenate([a_t, r_t], axis=0)
        sb, sk = scores(ar, b_t), scores(ar, k_t)
        p_ab = jnp.where(strict, sb[:c_len], 0.0)
        p_ak = jnp.where(strict, sk[:c_len], 0.0)
        p_rb = jnp.where(incl, sb[c_len:], 0.0)
        p_rk = jnp.where(incl, sk[c_len:], 0.0)
        yield

        a8 = jnp.where(same(8), p_ab, 0.0)
        e1 = ident + a8
        q2 = apply(a8, a8)
        x1 = apply(p_ak, v)
        y2 = apply(p_rk, v)
        yield
        tq = apply(q2, jnp.concatenate([e1, q2], axis=1))
        t1 = e1 + tq[:, :lanes]
        yield
        t_inv = t1 + apply(tq[:, lanes:], t1)
        yield
        for n in (16, 32, 64):
            off = jnp.where(same(n) & jnp.logical_not(same(n // 2)), p_ab, 0.0)
            tmp = apply(off, t_inv)
            yield
            t_inv = t_inv + apply(t_inv, tmp)
            yield

        av = apply(t_inv, jnp.concatenate([a_t, x1], axis=1))
        yield
        ry = apply(p_rb, av)
        r2 = r_t + ry[:, :lanes]
        y2 = y2 + ry[:, lanes:]
        mn = mm(b_h.T, av)
        m_c = jnp.where(bmask_f, mn[:, :lanes], 0.0) + jnp.where(eye, jnp.exp(tot), 0.0)
        n_c = jnp.where(bmask_f, mn[:, lanes:] + mm(k_h.T, v), 0.0)
        yield

        h0 = h_ref[g]
        y_ref[0, :, sl] = mm(r2, h0) + y2
        h_ref[g] = mm(m_c, h0) + n_c

    chains = [chain(g) for g in range(RW_DIM // lanes)]
    while all([next(c, False) is None for c in chains]):
        pass


def _scan(r, cs, k, v, a, b):
    bsz, t, _ = r.shape
    spec = pl.BlockSpec((1, SCAN_CHUNK, RW_DIM), lambda i, c: (i, c, 0))
    return pl.pallas_call(
        _scan_kernel,
        grid=(bsz, t // SCAN_CHUNK),
        in_specs=[spec] * 6,
        out_specs=spec,
        out_shape=jax.ShapeDtypeStruct((bsz, t, RW_DIM), F32),
        scratch_shapes=[pltpu.VMEM((RW_DIM // SCAN_LANES, SCAN_LANES, SCAN_LANES), F32)],
        compiler_params=_cparams(("parallel", "arbitrary")),
        name="wkv7_scan",
    )(r, cs, k, v, a, b)


def _rwpost_kernel(y_ref, bonus_ref, g_ref, lng_ref, lnb_ref, o_ref):
    y = y_ref[...]
    ym = _head_sum(y) * (1.0 / RW_HEAD)
    yc = y - ym
    yv = _head_sum(yc * yc) * (1.0 / RW_HEAD)
    out = yc * lax.rsqrt(yv + RW_GN_EPS) * lng_ref[...] + lnb_ref[...] + bonus_ref[...]
    o_ref[...] = (out * g_ref[...]).astype(BF16)


def _rwpost(y, bonus, g, ln_g, ln_b, tt=512):
    m = y.shape[0]
    spec = pl.BlockSpec((tt, RW_DIM), lambda i: (i, 0))
    vec = pl.BlockSpec((1, RW_DIM), lambda i: (0, 0))
    return pl.pallas_call(
        _rwpost_kernel,
        grid=(m // tt,),
        in_specs=[spec, spec, spec, vec, vec],
        out_specs=spec,
        out_shape=jax.ShapeDtypeStruct((m, RW_DIM), BF16),
        compiler_params=_cparams(("parallel",)),
        name="rwpost",
    )(y, bonus, g, ln_g, ln_b)


def _conv_kernel(a_ref, gate_ref, dw_ref, db_ref, lng_ref, lnb_ref, o_ref, buf_ref, y_ref):
    tt = a_ref.shape[1]
    rb, cb = 64, 128

    @pl.when(pl.program_id(1) == 0)
    def _():
        buf_ref[0:CV_HALO, :] = jnp.zeros((CV_HALO, CV_DIM), F32)

    buf_ref[CV_HALO:CV_HALO + tt, :] = a_ref[0] * _sigmoid(gate_ref[0])
    first = CV_HALO - (CV_WIDTH - 1)
    for r0 in range(0, tt, rb):
        for c0 in range(0, CV_DIM, cb):
            acc = jnp.broadcast_to(db_ref[:, c0:c0 + cb], (rb, cb))
            for j in range(CV_WIDTH):
                acc = acc + dw_ref[j:j + 1, c0:c0 + cb] * buf_ref[first + j + r0:first + j + r0 + rb, c0:c0 + cb]
            y_ref[r0:r0 + rb, c0:c0 + cb] = acc
    buf_ref[0:CV_HALO, :] = buf_ref[tt:tt + CV_HALO, :]
    z = _layer_norm(y_ref[...], lng_ref[...], lnb_ref[...])
    o_ref[0] = (z * _sigmoid(z)).astype(BF16)


def _conv(p3, dw, db, ln_g, ln_b, tt=256):
    b, t, _ = p3.shape
    vec = pl.BlockSpec((1, CV_DIM), lambda i, j: (0, 0))
    return pl.pallas_call(
        _conv_kernel,
        grid=(b, t // tt),
        in_specs=[
            pl.BlockSpec((1, tt, CV_DIM), lambda i, j: (i, j, 0)),
            pl.BlockSpec((1, tt, CV_DIM), lambda i, j: (i, j, 1)),
            pl.BlockSpec((CV_HALO, CV_DIM), lambda i, j: (0, 0)),
            vec, vec, vec,
        ],
        out_specs=pl.BlockSpec((1, tt, CV_DIM), lambda i, j: (i, j, 0)),
        out_shape=jax.ShapeDtypeStruct((b, t, CV_DIM), BF16),
        scratch_shapes=[pltpu.VMEM((CV_HALO + tt, CV_DIM), F32), pltpu.VMEM((tt, CV_DIM), F32)],
        compiler_params=_cparams(("parallel", "arbitrary")),
        name="conv",
    )(p3, p3, dw, db, ln_g, ln_b)


def _gmlp_kernel(u_ref, v_ref, lng_ref, lnb_ref, ws_ref, bs_ref, o_ref):
    tt = u_ref.shape[1]
    n_chunks = tt // SG_CHUNK
    grp = SG_DIM // SG_GROUPS
    u = _gelu(u_ref[0])
    v = _gelu(v_ref[0])
    vb = _layer_norm(v, lng_ref[...], lnb_ref[...]).astype(BF16)
    tri = (lax.broadcasted_iota(jnp.int32, (SG_CHUNK, SG_CHUNK), 1)
           <= lax.broadcasted_iota(jnp.int32, (SG_CHUNK, SG_CHUNK), 0))
    for g in range(SG_GROUPS):
        lanes = slice(g * grp, (g + 1) * grp)
        wg = jnp.where(tri, ws_ref[g], 0.0).astype(BF16)
        vg = jnp.concatenate([vb[c * SG_CHUNK:(c + 1) * SG_CHUNK, lanes] for c in range(n_chunks)], axis=1)
        sg = jnp.dot(wg, vg, preferred_element_type=F32)
        bias = bs_ref[:, g:g + 1]
        for c in range(n_chunks):
            rows = slice(c * SG_CHUNK, (c + 1) * SG_CHUNK)
            s = sg[:, c * grp:(c + 1) * grp] + bias
            o_ref[0, rows, lanes] = (u[rows, lanes] * s).astype(BF16)


def _gmlp(p3, ln_g, ln_b, ws, bs_t, tt=512):
    b, t, _ = p3.shape
    vec = pl.BlockSpec((1, SG_DIM), lambda i, j: (0, 0))
    return pl.pallas_call(
        _gmlp_kernel,
        grid=(b, t // tt),
        in_specs=[
            pl.BlockSpec((1, tt, SG_DIM), lambda i, j: (i, j, 2)),
            pl.BlockSpec((1, tt, SG_DIM), lambda i, j: (i, j, 3)),
            vec, vec,
            pl.BlockSpec((SG_GROUPS, SG_CHUNK, SG_CHUNK), lambda i, j: (0, 0, 0)),
            pl.BlockSpec((SG_CHUNK, SG_GROUPS), lambda i, j: (0, 0)),
        ],
        out_specs=pl.BlockSpec((1, tt, SG_DIM), lambda i, j: (i, j, 0)),
        out_shape=jax.ShapeDtypeStruct((b, t, SG_DIM), BF16),
        compiler_params=_cparams(("parallel", "parallel")),
        name="gmlp",
    )(p3, p3, ln_g, ln_b, ws, bs_t)


def _pad_rows(w, start, rows):
    return jnp.zeros((rows, w.shape[1]), BF16).at[start:start + w.shape[0]].set(w.astype(BF16))


def kernel(x, norm_g, ffn_w1, ffn_w3, ffn_w2, ev_w_in, ev_mu, pool_w, pool_scale, rw_w0, rw_w2, rw_a0, rw_a2,
           rw_g2, rw_kk, rw_ka, rw_rk, rw_ln_g, rw_ln_b, ev_w_out, od_w_in, cv_dw, cv_db, cv_ln_g, cv_ln_b,
           sg_ln_g, sg_ln_b, sg_ws, sg_b, od_w_out):
    bsz, t, d = x.shape
    m = bsz * t
    depth = norm_g.shape[0]
    row = lambda vct: vct.reshape(1, -1)

    w1, w3, w2 = ffn_w1.astype(BF16), ffn_w3.astype(BF16), ffn_w2.astype(BF16)
    h = x.reshape(m, d)
    for layer in range(depth):
        g = norm_g[layer]
        h = _ffn(h, row(g[0]), row(g[1]), w1, w3, w2, layer, 0)
        if layer % 2 == 0:
            e = layer // 2
            w_in = jnp.pad(ev_w_in[e], ((0, 0), (0, EV_IN_PAD - EV_IN))).astype(BF16)
            p3 = _inproj(h, row(g[2]), w_in, tn=EV_IN_PAD // 3).reshape(bsz, t, EV_IN_PAD)
            m1 = _pool(p3, pool_w[e].astype(BF16), row(pool_scale[e])).reshape(m, POOL_DIM)
            mu = ev_mu[e]
            mus = (row(mu[0:RW_DIM]), row(mu[RW_DIM:2 * RW_DIM]), row(mu[2 * RW_DIM:3 * RW_DIM]),
                   row(jnp.pad(mu[3 * RW_DIM:], (0, LORA_PAD - (EV_IN - POOL_DIM - 3 * RW_DIM)))))
            r, lw, k, v, a, b, gate, bonus = _rwprep(
                p3, mus, row(rw_w0[e]), _pad_rows(rw_w2[e], 0, 128), row(rw_a0[e]),
                _pad_rows(rw_a2[e], RW_DECAY_LORA, 128), _pad_rows(rw_g2[e], 0, 256),
                row(rw_kk[e]), row(rw_ka[e]), row(rw_rk[e]))
            y = _scan(r, lw, k, v, a, b)
            m2 = _rwpost(y.reshape(m, RW_DIM), bonus.reshape(m, RW_DIM), gate.reshape(m, RW_DIM),
                         row(rw_ln_g[e]), row(rw_ln_b[e]))
            h = _outproj(m1, m2, ev_w_out[e].astype(BF16), h, row(g[3]))
        else:
            o = layer // 2
            p3 = _inproj(h, row(g[2]), od_w_in[o].astype(BF16), tn=OD_IN // 4).reshape(bsz, t, OD_IN)
            dw = jnp.pad(cv_dw[o], ((0, CV_HALO - CV_WIDTH), (0, 0)))
            m1 = _conv(p3, dw, row(cv_db[o]), row(cv_ln_g[o]), row(cv_ln_b[o])).reshape(m, CV_DIM)
            m2 = _gmlp(p3, row(sg_ln_g[o]), row(sg_ln_b[o]), sg_ws[o], sg_b[o].T).reshape(m, SG_DIM)
            h = _outproj(m1, m2, od_w_out[o].astype(BF16), h, row(g[3]))
        h = _ffn(h, row(g[4]), row(g[5]), w1, w3, w2, layer, 1)
    return h.reshape(bsz, t, d)
```

```python
import functools

import jax
import jax.numpy as jnp
from jax import lax
from jax.experimental import pallas as pl
from jax.experimental.pallas import tpu as pltpu

F32 = jnp.float32
BF16 = jnp.bfloat16

D_MODEL = 2048
D_FF = 5632
NORM_EPS = 1e-6
LN_EPS = 1e-5
POOL_DIM = 1024
POOL_WINDOWS = (2, 4, 8, 16)
POOL_GROUP_DIM = 256
RW_DIM = 1024
RW_HEAD = 64
RW_DECAY_LORA = 64
RW_A_LORA = 64
RW_GATE_LORA = 160
RW_GN_EPS = 64e-5
EV_IN = POOL_DIM + 3 * RW_DIM + RW_DECAY_LORA + RW_A_LORA + RW_GATE_LORA
LORA_PAD = 512
EV_IN_PAD = POOL_DIM + 3 * RW_DIM + LORA_PAD
CV_DIM = 1024
CV_WIDTH = 31
CV_HALO = 32
CV_TAIL = 8
SG_DIM = 1024
SG_CHUNK = 128
SG_GROUPS = 8
OD_IN = 2 * CV_DIM + 2 * SG_DIM

SCAN_CHUNK = 64
SCAN_LANES = 256
SCAN_HEADS = SCAN_LANES // RW_HEAD

V7X_VMEM_BYTES = 64 * 1024 * 1024
VMEM_LIMIT = V7X_VMEM_BYTES - 4 * 1024 * 1024


def _cparams(sem):
    return pltpu.CompilerParams(dimension_semantics=sem, vmem_limit_bytes=VMEM_LIMIT)


def _rms(x, g):
    return x * lax.rsqrt(jnp.mean(x * x, axis=-1, keepdims=True) + NORM_EPS) * g


def _layer_norm(x, g, b):
    mu = jnp.mean(x, axis=-1, keepdims=True)
    xc = x - mu
    var = jnp.mean(xc * xc, axis=-1, keepdims=True)
    return xc * lax.rsqrt(var + LN_EPS) * g + b


def _sigmoid(x):
    return jax.nn.sigmoid(x)


def _gelu(x):
    return 0.5 * x * (1.0 + lax.erf(x * (2.0 ** -0.5)))


def _head_mask(n):
    r = lax.broadcasted_iota(jnp.int32, (n, n), 0) // RW_HEAD
    c = lax.broadcasted_iota(jnp.int32, (n, n), 1) // RW_HEAD
    return r == c


def _bf16_terms(x, n):
    terms = []
    for _ in range(n - 1):
        t = x.astype(BF16)
        terms.append(t)
        x = x - t.astype(F32)
    terms.append(x.astype(BF16))
    return terms


def _head_sum(x):
    ones_bd = _head_mask(SCAN_LANES).astype(BF16)
    parts = []
    for i in range(0, x.shape[1], SCAN_LANES):
        hi, lo = _bf16_terms(x[:, i:i + SCAN_LANES], 2)
        parts.append(jnp.dot(hi, ones_bd, preferred_element_type=F32)
                     + jnp.dot(lo, ones_bd, preferred_element_type=F32))
    return jnp.concatenate(parts, axis=1)


FFN_SLICES = 8


def _ffn_kernel(*refs, cast_next):
    hp_ref, hn_ref, gpre_ref, gpost_ref, w1_ref, w3_ref, w2_ref = refs[:7]
    if cast_next:
        f32_refs, o_ref, bf16_refs = refs[7:10], refs[10], refs[11:14]
    else:
        f32_refs, o_ref, bf16_refs = (), refs[7], ()
    xn0_ref, xn1_ref, acc0_ref, acc1_ref = refs[-4:]
    i, j = pl.program_id(0), pl.program_id(1)
    last_i, last_j = pl.num_programs(0) - 1, pl.num_programs(1) - 1
    rows_per = o_ref.shape[0] // FFN_SLICES

    @pl.when((i == 0) & (j == 0))
    def _():
        xn0_ref[...] = _rms(hp_ref[...], gpre_ref[...]).astype(BF16)
        acc0_ref[...] = jnp.zeros_like(acc0_ref)
        acc1_ref[...] = jnp.zeros_like(acc1_ref)

    def step(xn_ref, acc_ref, xn_next_ref, acc_prev_ref):
        for src_ref, dst_ref in zip(f32_refs, bf16_refs, strict=True):
            dst_ref[...] = src_ref[...].astype(BF16)
        rows = pl.ds(pl.multiple_of(jnp.minimum(j, FFN_SLICES - 1) * rows_per, rows_per), rows_per)
        xn_next_ref[rows, :] = _rms(hn_ref[rows, :], gpre_ref[...]).astype(BF16)
        o_ref[rows, :] = hp_ref[rows, :] + 0.5 * _rms(acc_prev_ref[rows, :], gpost_ref[...])

        x = xn_ref[...]
        a = jnp.dot(x, w1_ref[...], preferred_element_type=F32)
        b = jnp.dot(x, w3_ref[...], preferred_element_type=F32)
        hid = (a * _sigmoid(a) * b).astype(BF16)
        acc_ref[...] = jnp.where(j > 0, acc_ref[...], 0.0) + jnp.dot(hid, w2_ref[...], preferred_element_type=F32)

        @pl.when((i == last_i) & (j == last_j))
        def _():
            o_ref[...] = hn_ref[...] + 0.5 * _rms(acc_ref[...], gpost_ref[...])

    @pl.when(i % 2 == 0)
    def _():
        step(xn0_ref, acc0_ref, xn1_ref, acc1_ref)

    @pl.when(i % 2 == 1)
    def _():
        step(xn1_ref, acc1_ref, xn0_ref, acc0_ref)


def _ffn(h, gpre, gpost, wb, nxt=None, tm=512, tf=512):
    m = h.shape[0]
    n_i, n_j = m // tm, D_FF // tf
    assert n_j >= FFN_SLICES and tm % FFN_SLICES == 0
    in_specs = [
        pl.BlockSpec((tm, D_MODEL), lambda i, j: (jnp.maximum(i - 1, 0), 0)),
        pl.BlockSpec((tm, D_MODEL), lambda i, j: (jnp.minimum(i + 1, n_i - 1), 0)),
        pl.BlockSpec((1, D_MODEL), lambda i, j: (0, 0)),
        pl.BlockSpec((1, D_MODEL), lambda i, j: (0, 0)),
        pl.BlockSpec((D_MODEL, tf), lambda i, j: (0, j)),
        pl.BlockSpec((D_MODEL, tf), lambda i, j: (0, j)),
        pl.BlockSpec((tf, D_MODEL), lambda i, j: (j, 0)),
    ]
    out_specs = [pl.BlockSpec((tm, D_MODEL), lambda i, j: (jnp.where(j < FFN_SLICES, jnp.maximum(i - 1, 0), i), 0))]
    out_shape = [jax.ShapeDtypeStruct((m, D_MODEL), F32)]
    args = [h, h, gpre, gpost, *wb]
    if nxt is not None:
        f1, f3, f2, layer, idx = nxt
        dm, df = D_MODEL // n_i, D_FF // n_j
        in_specs += [
            pl.BlockSpec((None, None, dm, df), lambda i, j: (layer, idx, i, j)),
            pl.BlockSpec((None, None, dm, df), lambda i, j: (layer, idx, i, j)),
            pl.BlockSpec((None, None, df, dm), lambda i, j: (layer, idx, j, i)),
        ]
        out_specs += [
            pl.BlockSpec((dm, df), lambda i, j: (i, j)),
            pl.BlockSpec((dm, df), lambda i, j: (i, j)),
            pl.BlockSpec((df, dm), lambda i, j: (j, i)),
        ]
        out_shape += [jax.ShapeDtypeStruct((D_MODEL, D_FF), BF16), jax.ShapeDtypeStruct((D_MODEL, D_FF), BF16),
                      jax.ShapeDtypeStruct((D_FF, D_MODEL), BF16)]
        args += [f1, f3, f2]
    outs = pl.pallas_call(
        functools.partial(_ffn_kernel, cast_next=nxt is not None),
        grid=(n_i, n_j),
        in_specs=in_specs,
        out_specs=out_specs,
        out_shape=out_shape,
        scratch_shapes=[pltpu.VMEM((tm, D_MODEL), BF16), pltpu.VMEM((tm, D_MODEL), BF16),
                        pltpu.VMEM((tm, D_MODEL), F32), pltpu.VMEM((tm, D_MODEL), F32)],
        compiler_params=_cparams(("arbitrary", "arbitrary")),
        name="ffn",
    )(*args)
    return outs[0], (tuple(outs[1:]) if nxt is not None else None)


def _inproj_kernel(h_ref, g_ref, w_ref, o_ref, xn_ref):
    @pl.when(pl.program_id(1) == 0)
    def _():
        xn_ref[...] = _rms(h_ref[...], g_ref[...]).astype(BF16)

    o_ref[...] = jnp.dot(xn_ref[...], w_ref[...], preferred_element_type=F32)


def _inproj(h, g, w, tn, tm=1024):
    m = h.shape[0]
    n = w.shape[1]
    return pl.pallas_call(
        _inproj_kernel,
        grid=(m // tm, n // tn),
        in_specs=[
            pl.BlockSpec((tm, D_MODEL), lambda i, j: (i, 0)),
            pl.BlockSpec((1, D_MODEL), lambda i, j: (0, 0)),
            pl.BlockSpec((D_MODEL, tn), lambda i, j: (0, j)),
        ],
        out_specs=pl.BlockSpec((tm, tn), lambda i, j: (i, j)),
        out_shape=jax.ShapeDtypeStruct((m, n), F32),
        scratch_shapes=[pltpu.VMEM((tm, D_MODEL), BF16)],
        compiler_params=_cparams(("parallel", "arbitrary")),
        name="inproj",
    )(h, g, w)


def _outproj_kernel(x1_ref, x2_ref, w_ref, h_ref, g_ref, o_ref):
    n1 = x1_ref.shape[1]
    m = jnp.dot(x1_ref[...], w_ref[0:n1, :], preferred_element_type=F32)
    m = m + jnp.dot(x2_ref[...], w_ref[n1:, :], preferred_element_type=F32)
    o_ref[...] = h_ref[...] + _rms(m, g_ref[...])


def _outproj(x1, x2, w, h, g, tm=512):
    m = h.shape[0]
    n1, n2 = x1.shape[1], x2.shape[1]
    return pl.pallas_call(
        _outproj_kernel,
        grid=(m // tm,),
        in_specs=[
            pl.BlockSpec((tm, n1), lambda i: (i, 0)),
            pl.BlockSpec((tm, n2), lambda i: (i, 0)),
            pl.BlockSpec((n1 + n2, D_MODEL), lambda i: (0, 0)),
            pl.BlockSpec((tm, D_MODEL), lambda i: (i, 0)),
            pl.BlockSpec((1, D_MODEL), lambda i: (0, 0)),
        ],
        out_specs=pl.BlockSpec((tm, D_MODEL), lambda i: (i, 0)),
        out_shape=jax.ShapeDtypeStruct((m, D_MODEL), F32),
        compiler_params=_cparams(("parallel",)),
        name="outproj",
    )(x1, x2, w, h, g)


def _shift_rows(x, k):
    row = lax.broadcasted_iota(jnp.int32, x.shape, 0)
    return jnp.where(row >= k, pltpu.roll(x, k, axis=0), 0.0)


def _pool_kernel(p_ref, w_ref, s_ref, o_ref):
    gi = pl.program_id(1)
    x = p_ref[0]
    s2 = x + _shift_rows(x, 1)
    s4 = s2 + _shift_rows(s2, 2)
    s8 = s4 + _shift_rows(s4, 4)
    s16 = s8 + _shift_rows(s8, 8)
    wsum = jnp.where(gi == 0, s2, jnp.where(gi == 1, s4, jnp.where(gi == 2, s8, s16)))
    win = jnp.left_shift(2, gi)
    t = lax.broadcasted_iota(jnp.int32, (x.shape[0], 1), 0)
    cnt = jnp.minimum(t + 1, win).astype(F32)
    d = (wsum / cnt - x).astype(BF16)
    o_ref[0] = (jnp.dot(d, w_ref[0], preferred_element_type=F32) * s_ref[...]).astype(BF16)


def _pool(p3, pool_w, pool_scale):
    b, t, _ = p3.shape
    n_grp = len(POOL_WINDOWS)
    return pl.pallas_call(
        _pool_kernel,
        grid=(b, n_grp),
        in_specs=[
            pl.BlockSpec((1, t, POOL_GROUP_DIM), lambda i, g: (i, 0, g)),
            pl.BlockSpec((1, POOL_GROUP_DIM, POOL_GROUP_DIM), lambda i, g: (g, 0, 0)),
            pl.BlockSpec((1, POOL_GROUP_DIM), lambda i, g: (0, g)),
        ],
        out_specs=pl.BlockSpec((1, t, POOL_GROUP_DIM), lambda i, g: (i, 0, g)),
        out_shape=jax.ShapeDtypeStruct((b, t, POOL_DIM), BF16),
        compiler_params=_cparams(("parallel", "parallel")),
        name="pool",
    )(p3, pool_w, pool_scale)


def _rwprep_kernel(r_ref, k_ref, v_ref, l_ref, rp_ref, kp_ref, vp_ref, lp_ref,
                   mur_ref, muk_ref, muv_ref, mul_ref, w0_ref, w2_ref, a0_ref, a2_ref, g2_ref,
                   kk_ref, ka_ref, rk_ref,
                   ro_ref, cso_ref, ko_ref, vo_ref, ao_ref, bo_ref, go_ref, bonus_ref):
    ti = pl.program_id(1)

    def token_shift(cur_ref, prev_ref, mu_ref):
        x = cur_ref[0]
        prev_row = jnp.where(ti > 0, prev_ref[0, 7:8, :], 0.0)
        row = lax.broadcasted_iota(jnp.int32, x.shape, 0)
        xs = jnp.where(row == 0, prev_row, pltpu.roll(x, 1, axis=0))
        return x + (xs - x) * mu_ref[...]

    r = token_shift(r_ref, rp_ref, mur_ref)
    k = token_shift(k_ref, kp_ref, muk_ref)
    v = token_shift(v_ref, vp_ref, muv_ref)
    lo = token_shift(l_ref, lp_ref, mul_ref)

    lo_wa = lo[:, 0:128]
    xw = w0_ref[...] + jnp.dot(jnp.tanh(lo_wa).astype(BF16), w2_ref[...], preferred_element_type=F32)
    nx = -xw
    softplus = jnp.maximum(nx, 0.0) + jnp.log1p(jnp.exp(-jnp.abs(nx)))
    w_log = -softplus - 0.5
    lw = -jnp.exp(w_log)
    tt = lw.shape[0]
    ci = lax.broadcasted_iota(jnp.int32, (tt, tt), 0)
    cj = lax.broadcasted_iota(jnp.int32, (tt, tt), 1)
    chunk_tri = ((ci // SCAN_CHUNK == cj // SCAN_CHUNK) & (cj <= ci)).astype(BF16)
    cs = sum(jnp.dot(chunk_tri, term, preferred_element_type=F32) for term in _bf16_terms(lw, 3))
    cso_ref[0] = cs
    a = _sigmoid(a0_ref[...] + jnp.dot(lo_wa.astype(BF16), a2_ref[...], preferred_element_type=F32))
    go_ref[0] = jnp.dot(_sigmoid(lo[:, 128:384]).astype(BF16), g2_ref[...], preferred_element_type=F32)

    kk = k * kk_ref[...]
    kk = kk / jnp.maximum(jnp.sqrt(_head_sum(kk * kk)), 1e-12)
    k2 = k * (1.0 + (a - 1.0) * ka_ref[...])
    ro_ref[0] = r
    ko_ref[0] = k2
    vo_ref[0] = v
    ao_ref[0] = -kk * jnp.exp(cs - lw)
    bo_ref[0] = kk * a
    bonus_ref[0] = _head_sum(r * k2 * rk_ref[...]) * v


def _rwprep(p3, mu, w0, w2p, a0, a2p, g2p, kkp, kap, rkp, tt=256):
    b, t, _ = p3.shape
    nt = t // tt
    blk8 = tt // 8
    col = lambda c: (lambda i, j: (i, j, c))
    prev = lambda c: (lambda i, j: (i, jnp.maximum(j * blk8 - 1, 0), c))
    vec = lambda n: pl.BlockSpec((1, n), lambda i, j: (0, 0))
    lora_col = (POOL_DIM + 3 * RW_DIM) // LORA_PAD
    in_specs = [
        pl.BlockSpec((1, tt, RW_DIM), col(1)),
        pl.BlockSpec((1, tt, RW_DIM), col(2)),
        pl.BlockSpec((1, tt, RW_DIM), col(3)),
        pl.BlockSpec((1, tt, LORA_PAD), col(lora_col)),
        pl.BlockSpec((1, 8, RW_DIM), prev(1)),
        pl.BlockSpec((1, 8, RW_DIM), prev(2)),
        pl.BlockSpec((1, 8, RW_DIM), prev(3)),
        pl.BlockSpec((1, 8, LORA_PAD), prev(lora_col)),
        vec(RW_DIM), vec(RW_DIM), vec(RW_DIM), vec(LORA_PAD),
        vec(RW_DIM),
        pl.BlockSpec((128, RW_DIM), lambda i, j: (0, 0)),
        vec(RW_DIM),
        pl.BlockSpec((128, RW_DIM), lambda i, j: (0, 0)),
        pl.BlockSpec((256, RW_DIM), lambda i, j: (0, 0)),
        vec(RW_DIM), vec(RW_DIM), vec(RW_DIM),
    ]
    out_spec = pl.BlockSpec((1, tt, RW_DIM), lambda i, j: (i, j, 0))
    out_sds = jax.ShapeDtypeStruct((b, t, RW_DIM), F32)
    return pl.pallas_call(
        _rwprep_kernel,
        grid=(b, nt),
        in_specs=in_specs,
        out_specs=[out_spec] * 8,
        out_shape=[out_sds] * 8,
        compiler_params=_cparams(("parallel", "parallel")),
        name="rwprep",
    )(p3, p3, p3, p3, p3, p3, p3, p3, mu[0], mu[1], mu[2], mu[3], w0, w2p, a0, a2p, g2p, kkp, kap, rkp)


def _scan_kernel(r_ref, cs_ref, k_ref, v_ref, a_ref, b_ref, y_ref, h_ref):
    c_len, lanes = SCAN_CHUNK, SCAN_LANES

    @pl.when(pl.program_id(1) == 0)
    def _():
        h_ref[...] = jnp.zeros_like(h_ref)

    bmask_f = _head_mask(lanes)
    bmask = bmask_f.astype(BF16)
    bmask2 = jnp.concatenate([bmask, bmask], axis=1)
    row = lax.broadcasted_iota(jnp.int32, (c_len, lanes), 0)
    src = lax.broadcasted_iota(jnp.int32, (c_len, lanes), 1) % c_len
    strict, incl = src < row, src <= row
    ident = jnp.where(src == row, 1.0, 0.0)
    same = lambda n: (row // n) == (src // n)
    eye = (lax.broadcasted_iota(jnp.int32, (lanes, lanes), 0)
           == lax.broadcasted_iota(jnp.int32, (lanes, lanes), 1))

    def mm(x, y):
        return jnp.dot(x.astype(BF16), y.astype(BF16), preferred_element_type=F32)

    def stack(x):
        xb = jnp.concatenate([x.astype(BF16)] * SCAN_HEADS, axis=0)
        return xb * (bmask if x.shape[1] == lanes else bmask2)

    def apply(p, x):
        return jnp.dot(p.astype(BF16), stack(x), preferred_element_type=F32)

    def scores(x, y):
        return lax.dot_general(x.astype(BF16), stack(y), (((1,), (1,)), ((), ())), preferred_element_type=F32)

    n_grp = RW_DIM // lanes

    def chain(bi, g):
        sl = slice(g * lanes, (g + 1) * lanes)
        r, cs, k, v, a_t, b = (ref[bi, :, sl] for ref in (r_ref, cs_ref, k_ref, v_ref, a_ref, b_ref))
        tot = cs[c_len - 1:c_len, :]
        e_neg, e_rem = jnp.exp(-cs), jnp.exp(tot - cs)
        r_t = r * jnp.exp(cs)
        b_t, k_t = b * e_neg, k * e_neg
        b_h, k_h = b * e_rem, k * e_rem

        ar = jnp.concatenate([a_t, r_t], axis=0)
        sb, sk = scores(ar, b_t), scores(ar, k_t)
        p_ab = jnp.where(strict, sb[:c_len], 0.0)
        p_ak = jnp.where(strict, sk[:c_len], 0.0)
        p_rb = jnp.where(incl, sb[c_len:], 0.0)
        p_rk = jnp.where(incl, sk[c_len:], 0.0)
        yield

        a8 = jnp.where(same(8), p_ab, 0.0)
        e1 = ident + a8
        q2 = apply(a8, a8)
        x1 = apply(p_ak, v)
        y2 = apply(p_rk, v)
        yield
        tq = apply(q2, jnp.concatenate([e1, q2], axis=1))
        t1 = e1 + tq[:, :lanes]
        yield
        t_inv = t1 + apply(tq[:, lanes:], t1)
        yield
        for n in (16, 32, 64):
            off = jnp.where(same(n) & jnp.logical_not(same(n // 2)), p_ab, 0.0)
            tmp = apply(off, t_inv)
            yield
            t_inv = t_inv + apply(t_inv, tmp)
            yield

        av = apply(t_inv, jnp.concatenate([a_t, x1], axis=1))
        yield
        ry = apply(p_rb, av)
        r2 = r_t + ry[:, :lanes]
        y2 = y2 + ry[:, lanes:]
        mn = mm(b_h.T, av)
        m_c = jnp.where(bmask_f, mn[:, :lanes], 0.0) + jnp.where(eye, jnp.exp(tot), 0.0)
        n_c = jnp.where(bmask_f, mn[:, lanes:] + mm(k_h.T, v), 0.0)
        yield

        h0 = h_ref[bi * n_grp + g]
        y_ref[bi, :, sl] = mm(r2, h0) + y2
        h_ref[bi * n_grp + g] = mm(m_c, h0) + n_c

    chains = [chain(bi, g) for bi in range(r_ref.shape[0]) for g in range(n_grp)]
    while all([next(c, False) is None for c in chains]):
        pass


def _scan(r, cs, k, v, a, b, nb=2):
    bsz, t, _ = r.shape
    spec = pl.BlockSpec((nb, SCAN_CHUNK, RW_DIM), lambda i, c: (i, c, 0))
    return pl.pallas_call(
        _scan_kernel,
        grid=(bsz // nb, t // SCAN_CHUNK),
        in_specs=[spec] * 6,
        out_specs=spec,
        out_shape=jax.ShapeDtypeStruct((bsz, t, RW_DIM), F32),
        scratch_shapes=[pltpu.VMEM((nb * (RW_DIM // SCAN_LANES), SCAN_LANES, SCAN_LANES), F32)],
        compiler_params=_cparams(("parallel", "arbitrary")),
        name="wkv7_scan",
    )(r, cs, k, v, a, b)


def _rwpost_kernel(y_ref, bonus_ref, g_ref, lng_ref, lnb_ref, o_ref):
    y = y_ref[...]
    ym = _head_sum(y) * (1.0 / RW_HEAD)
    yc = y - ym
    yv = _head_sum(yc * yc) * (1.0 / RW_HEAD)
    out = yc * lax.rsqrt(yv + RW_GN_EPS) * lng_ref[...] + lnb_ref[...] + bonus_ref[...]
    o_ref[...] = (out * g_ref[...]).astype(BF16)


def _rwpost(y, bonus, g, ln_g, ln_b, tt=512):
    m = y.shape[0]
    spec = pl.BlockSpec((tt, RW_DIM), lambda i: (i, 0))
    vec = pl.BlockSpec((1, RW_DIM), lambda i: (0, 0))
    return pl.pallas_call(
        _rwpost_kernel,
        grid=(m // tt,),
        in_specs=[spec, spec, spec, vec, vec],
        out_specs=spec,
        out_shape=jax.ShapeDtypeStruct((m, RW_DIM), BF16),
        compiler_params=_cparams(("parallel",)),
        name="rwpost",
    )(y, bonus, g, ln_g, ln_b)


def _conv_kernel(a_ref, gate_ref, dw_ref, db_ref, lng_ref, lnb_ref, o_ref, buf_ref, y_ref):
    tt = a_ref.shape[1]
    rb, cb = 128, 128

    @pl.when(pl.program_id(1) == 0)
    def _():
        buf_ref[0:CV_HALO, :] = jnp.zeros((CV_HALO, CV_DIM), F32)
        buf_ref[CV_HALO + tt:, :] = jnp.zeros((CV_TAIL, CV_DIM), F32)

    buf_ref[CV_HALO:CV_HALO + tt, :] = a_ref[0] * _sigmoid(gate_ref[0])
    first = CV_HALO - (CV_WIDTH - 1)
    ext = rb + 16
    for r0 in range(0, tt, rb):
        for c0 in range(0, CV_DIM, cb):
            acc = jnp.broadcast_to(db_ref[:, c0:c0 + cb], (rb, cb))
            for s in range(8):
                zs = None
                for j in range(s, CV_WIDTH, 8):
                    term = dw_ref[j:j + 1, c0:c0 + cb] * buf_ref[r0 + j - s:r0 + j - s + ext, c0:c0 + cb]
                    zs = term if zs is None else zs + term
                acc = acc + zs[first + s:first + s + rb]
            y_ref[r0:r0 + rb, c0:c0 + cb] = acc
    buf_ref[0:CV_HALO, :] = buf_ref[tt:tt + CV_HALO, :]
    z = _layer_norm(y_ref[...], lng_ref[...], lnb_ref[...])
    o_ref[0] = (z * _sigmoid(z)).astype(BF16)


def _conv(p3, dw, db, ln_g, ln_b, tt=256):
    b, t, _ = p3.shape
    vec = pl.BlockSpec((1, CV_DIM), lambda i, j: (0, 0))
    return pl.pallas_call(
        _conv_kernel,
        grid=(b, t // tt),
        in_specs=[
            pl.BlockSpec((1, tt, CV_DIM), lambda i, j: (i, j, 0)),
            pl.BlockSpec((1, tt, CV_DIM), lambda i, j: (i, j, 1)),
            pl.BlockSpec((CV_HALO, CV_DIM), lambda i, j: (0, 0)),
            vec, vec, vec,
        ],
        out_specs=pl.BlockSpec((1, tt, CV_DIM), lambda i, j: (i, j, 0)),
        out_shape=jax.ShapeDtypeStruct((b, t, CV_DIM), BF16),
        scratch_shapes=[pltpu.VMEM((CV_HALO + tt + CV_TAIL, CV_DIM), F32), pltpu.VMEM((tt, CV_DIM), F32)],
        compiler_params=_cparams(("parallel", "arbitrary")),
        name="conv",
    )(p3, p3, dw, db, ln_g, ln_b)


def _gmlp_kernel(u_ref, v_ref, lng_ref, lnb_ref, ws_ref, bs_ref, o_ref):
    tt = u_ref.shape[1]
    n_chunks = tt // SG_CHUNK
    grp = SG_DIM // SG_GROUPS
    u = _gelu(u_ref[0])
    v = _gelu(v_ref[0])
    vb = _layer_norm(v, lng_ref[...], lnb_ref[...]).astype(BF16)
    tri = (lax.broadcasted_iota(jnp.int32, (SG_CHUNK, SG_CHUNK), 1)
           <= lax.broadcasted_iota(jnp.int32, (SG_CHUNK, SG_CHUNK), 0))
    for g in range(SG_GROUPS):
        lanes = slice(g * grp, (g + 1) * grp)
        wg = jnp.where(tri, ws_ref[g], 0.0).astype(BF16)
        vg = jnp.concatenate([vb[c * SG_CHUNK:(c + 1) * SG_CHUNK, lanes] for c in range(n_chunks)], axis=1)
        sg = jnp.dot(wg, vg, preferred_element_type=F32)
        bias = bs_ref[:, g:g + 1]
        for c in range(n_chunks):
            rows = slice(c * SG_CHUNK, (c + 1) * SG_CHUNK)
            s = sg[:, c * grp:(c + 1) * grp] + bias
            o_ref[0, rows, lanes] = (u[rows, lanes] * s).astype(BF16)


def _gmlp(p3, ln_g, ln_b, ws, bs_t, tt=512):
    b, t, _ = p3.shape
    vec = pl.BlockSpec((1, SG_DIM), lambda i, j: (0, 0))
    return pl.pallas_call(
        _gmlp_kernel,
        grid=(b, t // tt),
        in_specs=[
            pl.BlockSpec((1, tt, SG_DIM), lambda i, j: (i, j, 2)),
            pl.BlockSpec((1, tt, SG_DIM), lambda i, j: (i, j, 3)),
            vec, vec,
            pl.BlockSpec((SG_GROUPS, SG_CHUNK, SG_CHUNK), lambda i, j: (0, 0, 0)),
            pl.BlockSpec((SG_CHUNK, SG_GROUPS), lambda i, j: (0, 0)),
        ],
        out_specs=pl.BlockSpec((1, tt, SG_DIM), lambda i, j: (i, j, 0)),
        out_shape=jax.ShapeDtypeStruct((b, t, SG_DIM), BF16),
        compiler_params=_cparams(("parallel", "parallel")),
        name="gmlp",
    )(p3, p3, ln_g, ln_b, ws, bs_t)


def _pad_rows(w, start, rows):
    return jnp.zeros((rows, w.shape[1]), BF16).at[start:start + w.shape[0]].set(w.astype(BF16))


def kernel(x, norm_g, ffn_w1, ffn_w3, ffn_w2, ev_w_in, ev_mu, pool_w, pool_scale, rw_w0, rw_w2, rw_a0, rw_a2,
           rw_g2, rw_kk, rw_ka, rw_rk, rw_ln_g, rw_ln_b, ev_w_out, od_w_in, cv_dw, cv_db, cv_ln_g, cv_ln_b,
           sg_ln_g, sg_ln_b, sg_ws, sg_b, od_w_out):
    bsz, t, d = x.shape
    m = bsz * t
    depth = norm_g.shape[0]
    row = lambda vct: vct.reshape(1, -1)

    wb = tuple(w[0, 0].astype(BF16) for w in (ffn_w1, ffn_w3, ffn_w2))
    h = x.reshape(m, d)
    for layer in range(depth):
        g = norm_g[layer]
        h, wb = _ffn(h, row(g[0]), row(g[1]), wb, nxt=(ffn_w1, ffn_w3, ffn_w2, layer, 1))
        if layer % 2 == 0:
            e = layer // 2
            w_in = jnp.pad(ev_w_in[e], ((0, 0), (0, EV_IN_PAD - EV_IN))).astype(BF16)
            p3 = _inproj(h, row(g[2]), w_in, tn=EV_IN_PAD // 3).reshape(bsz, t, EV_IN_PAD)
            m1 = _pool(p3, pool_w[e].astype(BF16), row(pool_scale[e])).reshape(m, POOL_DIM)
            mu = ev_mu[e]
            mus = (row(mu[0:RW_DIM]), row(mu[RW_DIM:2 * RW_DIM]), row(mu[2 * RW_DIM:3 * RW_DIM]),
                   row(jnp.pad(mu[3 * RW_DIM:], (0, LORA_PAD - (EV_IN - POOL_DIM - 3 * RW_DIM)))))
            r, lw, k, v, a, b, gate, bonus = _rwprep(
                p3, mus, row(rw_w0[e]), _pad_rows(rw_w2[e], 0, 128), row(rw_a0[e]),
                _pad_rows(rw_a2[e], RW_DECAY_LORA, 128), _pad_rows(rw_g2[e], 0, 256),
                row(rw_kk[e]), row(rw_ka[e]), row(rw_rk[e]))
            y = _scan(r, lw, k, v, a, b)
            m2 = _rwpost(y.reshape(m, RW_DIM), bonus.reshape(m, RW_DIM), gate.reshape(m, RW_DIM),
                         row(rw_ln_g[e]), row(rw_ln_b[e]))
            h = _outproj(m1, m2, ev_w_out[e].astype(BF16), h, row(g[3]))
        else:
            o = layer // 2
            p3 = _inproj(h, row(g[2]), od_w_in[o].astype(BF16), tn=OD_IN // 4).reshape(bsz, t, OD_IN)
            dw = jnp.pad(cv_dw[o], ((0, CV_HALO - CV_WIDTH), (0, 0)))
            m1 = _conv(p3, dw, row(cv_db[o]), row(cv_ln_g[o]), row(cv_ln_b[o])).reshape(m, CV_DIM)
            m2 = _gmlp(p3, row(sg_ln_g[o]), row(sg_ln_b[o]), sg_ws[o], sg_b[o].T).reshape(m, SG_DIM)
            h = _outproj(m1, m2, od_w_out[o].astype(BF16), h, row(g[3]))
        nxt = (ffn_w1, ffn_w3, ffn_w2, layer + 1, 0) if layer + 1 < depth else None
        h, wb = _ffn(h, row(g[4]), row(g[5]), wb, nxt=nxt)
    return h.reshape(bsz, t, d)
```

```python
import functools

import jax
import jax.numpy as jnp
from jax import lax
from jax.experimental import pallas as pl
from jax.experimental.pallas import tpu as pltpu

F32 = jnp.float32
BF16 = jnp.bfloat16

D_MODEL = 2048
D_FF = 5632
NORM_EPS = 1e-6
LN_EPS = 1e-5
POOL_DIM = 1024
POOL_WINDOWS = (2, 4, 8, 16)
POOL_GROUP_DIM = 256
RW_DIM = 1024
RW_HEAD = 64
RW_DECAY_LORA = 64
RW_A_LORA = 64
RW_GATE_LORA = 160
RW_GN_EPS = 64e-5
EV_IN = POOL_DIM + 3 * RW_DIM + RW_DECAY_LORA + RW_A_LORA + RW_GATE_LORA
LORA_PAD = 512
EV_IN_PAD = POOL_DIM + 3 * RW_DIM + LORA_PAD
CV_DIM = 1024
CV_WIDTH = 31
CV_HALO = 32
CV_TAIL = 8
SG_DIM = 1024
SG_CHUNK = 128
SG_GROUPS = 8
OD_IN = 2 * CV_DIM + 2 * SG_DIM

SCAN_CHUNK = 64
SCAN_LANES = 256
SCAN_HEADS = SCAN_LANES // RW_HEAD

V7X_VMEM_BYTES = 64 * 1024 * 1024
VMEM_LIMIT = V7X_VMEM_BYTES - 4 * 1024 * 1024


def _cparams(sem):
    return pltpu.CompilerParams(dimension_semantics=sem, vmem_limit_bytes=VMEM_LIMIT)


def _rms(x, g):
    return x * lax.rsqrt(jnp.mean(x * x, axis=-1, keepdims=True) + NORM_EPS) * g


def _layer_norm(x, g, b):
    mu = jnp.mean(x, axis=-1, keepdims=True)
    xc = x - mu
    var = jnp.mean(xc * xc, axis=-1, keepdims=True)
    return xc * lax.rsqrt(var + LN_EPS) * g + b


def _sigmoid(x):
    return jax.nn.sigmoid(x)


def _gelu(x):
    return 0.5 * x * (1.0 + lax.erf(x * (2.0 ** -0.5)))


def _head_mask(n):
    r = lax.broadcasted_iota(jnp.int32, (n, n), 0) // RW_HEAD
    c = lax.broadcasted_iota(jnp.int32, (n, n), 1) // RW_HEAD
    return r == c


def _bf16_terms(x, n):
    terms = []
    for _ in range(n - 1):
        t = x.astype(BF16)
        terms.append(t)
        x = x - t.astype(F32)
    terms.append(x.astype(BF16))
    return terms


def _head_sum(x):
    ones_bd = _head_mask(SCAN_LANES).astype(BF16)
    parts = []
    for i in range(0, x.shape[1], SCAN_LANES):
        hi, lo = _bf16_terms(x[:, i:i + SCAN_LANES], 2)
        parts.append(jnp.dot(hi, ones_bd, preferred_element_type=F32)
                     + jnp.dot(lo, ones_bd, preferred_element_type=F32))
    return jnp.concatenate(parts, axis=1)


def _ffn_kernel(*refs, cast_next):
    h_ref, gpre_ref, gpost_ref, w1_ref, w3_ref, w2_ref = refs[:6]
    if cast_next:
        f32_refs, o_ref, bf16_refs = refs[6:9], refs[9], refs[10:13]
    else:
        f32_refs, o_ref, bf16_refs = (), refs[6], ()
    xn_ref = refs[-1]
    j = pl.program_id(1)

    @pl.when(j == 0)
    def _():
        xn_ref[...] = _rms(h_ref[...], gpre_ref[...]).astype(BF16)
        o_ref[...] = jnp.zeros_like(o_ref)

    for src_ref, dst_ref in zip(f32_refs, bf16_refs, strict=True):
        dst_ref[...] = src_ref[...].astype(BF16)
    x = xn_ref[...]
    a = jnp.dot(x, w1_ref[...], preferred_element_type=F32)
    b = jnp.dot(x, w3_ref[...], preferred_element_type=F32)
    hid = (a * _sigmoid(a) * b).astype(BF16)
    o_ref[...] += jnp.dot(hid, w2_ref[...], preferred_element_type=F32)

    @pl.when(j == pl.num_programs(1) - 1)
    def _():
        o_ref[...] = h_ref[...] + 0.5 * _rms(o_ref[...], gpost_ref[...])


def _ffn(h, gpre, gpost, wb, nxt=None, tm=1024, tf=256):
    m = h.shape[0]
    n_i, n_j = m // tm, D_FF // tf
    in_specs = [
        pl.BlockSpec((tm, D_MODEL), lambda i, j: (i, 0)),
        pl.BlockSpec((1, D_MODEL), lambda i, j: (0, 0)),
        pl.BlockSpec((1, D_MODEL), lambda i, j: (0, 0)),
        pl.BlockSpec((D_MODEL, tf), lambda i, j: (0, j)),
        pl.BlockSpec((D_MODEL, tf), lambda i, j: (0, j)),
        pl.BlockSpec((tf, D_MODEL), lambda i, j: (j, 0)),
    ]
    out_specs = [pl.BlockSpec((tm, D_MODEL), lambda i, j: (i, 0))]
    out_shape = [jax.ShapeDtypeStruct((m, D_MODEL), F32)]
    args = [h, gpre, gpost, *wb]
    if nxt is not None:
        f1, f3, f2, layer, idx = nxt
        dm, df = D_MODEL // n_i, D_FF // n_j
        in_specs += [
            pl.BlockSpec((None, None, dm, df), lambda i, j: (layer, idx, i, j)),
            pl.BlockSpec((None, None, dm, df), lambda i, j: (layer, idx, i, j)),
            pl.BlockSpec((None, None, df, dm), lambda i, j: (layer, idx, j, i)),
        ]
        out_specs += [
            pl.BlockSpec((dm, df), lambda i, j: (i, j)),
            pl.BlockSpec((dm, df), lambda i, j: (i, j)),
            pl.BlockSpec((df, dm), lambda i, j: (j, i)),
        ]
        out_shape += [jax.ShapeDtypeStruct((D_MODEL, D_FF), BF16), jax.ShapeDtypeStruct((D_MODEL, D_FF), BF16),
                      jax.ShapeDtypeStruct((D_FF, D_MODEL), BF16)]
        args += [f1, f3, f2]
    outs = pl.pallas_call(
        functools.partial(_ffn_kernel, cast_next=nxt is not None),
        grid=(n_i, n_j),
        in_specs=in_specs,
        out_specs=out_specs,
        out_shape=out_shape,
        scratch_shapes=[pltpu.VMEM((tm, D_MODEL), BF16)],
        compiler_params=_cparams(("parallel", "arbitrary")),
        name="ffn",
    )(*args)
    return outs[0], (tuple(outs[1:]) if nxt is not None else None)


def _inproj_kernel(h_ref, g_ref, w_ref, o_ref, xn_ref):
    @pl.when(pl.program_id(1) == 0)
    def _():
        xn_ref[...] = _rms(h_ref[...], g_ref[...]).astype(BF16)

    o_ref[...] = jnp.dot(xn_ref[...], w_ref[...], preferred_element_type=F32)


def _inproj(h, g, w, e, tn, tm=1024):
    m = h.shape[0]
    n = w.shape[2]
    return pl.pallas_call(
        _inproj_kernel,
        grid=(m // tm, n // tn),
        in_specs=[
            pl.BlockSpec((tm, D_MODEL), lambda i, j: (i, 0)),
            pl.BlockSpec((1, D_MODEL), lambda i, j: (0, 0)),
            pl.BlockSpec((None, D_MODEL, tn), lambda i, j: (e, 0, j)),
        ],
        out_specs=pl.BlockSpec((tm, tn), lambda i, j: (i, j)),
        out_shape=jax.ShapeDtypeStruct((m, n), F32),
        scratch_shapes=[pltpu.VMEM((tm, D_MODEL), BF16)],
        compiler_params=_cparams(("parallel", "arbitrary")),
        name="inproj",
    )(h, g, w)


def _outproj_kernel(x1_ref, x2_ref, w_ref, h_ref, g_ref, o_ref):
    n1 = x1_ref.shape[1]
    m = jnp.dot(x1_ref[...], w_ref[0:n1, :], preferred_element_type=F32)
    m = m + jnp.dot(x2_ref[...], w_ref[n1:, :], preferred_element_type=F32)
    o_ref[...] = h_ref[...] + _rms(m, g_ref[...])


def _outproj(x1, x2, w, e, h, g, tm=512):
    m = h.shape[0]
    n1, n2 = x1.shape[1], x2.shape[1]
    return pl.pallas_call(
        _outproj_kernel,
        grid=(m // tm,),
        in_specs=[
            pl.BlockSpec((tm, n1), lambda i: (i, 0)),
            pl.BlockSpec((tm, n2), lambda i: (i, 0)),
            pl.BlockSpec((None, n1 + n2, D_MODEL), lambda i: (e, 0, 0)),
            pl.BlockSpec((tm, D_MODEL), lambda i: (i, 0)),
            pl.BlockSpec((1, D_MODEL), lambda i: (0, 0)),
        ],
        out_specs=pl.BlockSpec((tm, D_MODEL), lambda i: (i, 0)),
        out_shape=jax.ShapeDtypeStruct((m, D_MODEL), F32),
        compiler_params=_cparams(("parallel",)),
        name="outproj",
    )(x1, x2, w, h, g)


def _shift_rows(x, k):
    row = lax.broadcasted_iota(jnp.int32, x.shape, 0)
    return jnp.where(row >= k, pltpu.roll(x, k, axis=0), 0.0)


def _pool_kernel(p_ref, w_ref, s_ref, o_ref):
    gi = pl.program_id(1)
    x = p_ref[0]
    s2 = x + _shift_rows(x, 1)
    s4 = s2 + _shift_rows(s2, 2)
    s8 = s4 + _shift_rows(s4, 4)
    s16 = s8 + _shift_rows(s8, 8)
    wsum = jnp.where(gi == 0, s2, jnp.where(gi == 1, s4, jnp.where(gi == 2, s8, s16)))
    win = jnp.left_shift(2, gi)
    t = lax.broadcasted_iota(jnp.int32, (x.shape[0], 1), 0)
    cnt = jnp.minimum(t + 1, win).astype(F32)
    d = (wsum / cnt - x).astype(BF16)
    o_ref[0] = (jnp.dot(d, w_ref[0], preferred_element_type=F32) * s_ref[...]).astype(BF16)


def _pool(p3, pool_w, pool_scale):
    b, t, _ = p3.shape
    n_grp = len(POOL_WINDOWS)
    return pl.pallas_call(
        _pool_kernel,
        grid=(b, n_grp),
        in_specs=[
            pl.BlockSpec((1, t, POOL_GROUP_DIM), lambda i, g: (i, 0, g)),
            pl.BlockSpec((1, POOL_GROUP_DIM, POOL_GROUP_DIM), lambda i, g: (g, 0, 0)),
            pl.BlockSpec((1, POOL_GROUP_DIM), lambda i, g: (0, g)),
        ],
        out_specs=pl.BlockSpec((1, t, POOL_GROUP_DIM), lambda i, g: (i, 0, g)),
        out_shape=jax.ShapeDtypeStruct((b, t, POOL_DIM), BF16),
        compiler_params=_cparams(("parallel", "parallel")),
        name="pool",
    )(p3, pool_w, pool_scale)


def _rwprep_kernel(r_ref, k_ref, v_ref, l_ref, rp_ref, kp_ref, vp_ref, lp_ref,
                   mur_ref, muk_ref, muv_ref, mul_ref, w0_ref, w2_ref, a0_ref, a2_ref, g2_ref,
                   kk_ref, ka_ref, rk_ref,
                   ro_ref, cso_ref, ko_ref, vo_ref, ao_ref, bo_ref, go_ref, bonus_ref):
    ti = pl.program_id(1)

    def token_shift(cur_ref, prev_ref, mu_ref):
        x = cur_ref[0]
        prev_row = jnp.where(ti > 0, prev_ref[0, 7:8, :], 0.0)
        row = lax.broadcasted_iota(jnp.int32, x.shape, 0)
        xs = jnp.where(row == 0, prev_row, pltpu.roll(x, 1, axis=0))
        return x + (xs - x) * mu_ref[...]

    r = token_shift(r_ref, rp_ref, mur_ref)
    k = token_shift(k_ref, kp_ref, muk_ref)
    v = token_shift(v_ref, vp_ref, muv_ref)
    lo = token_shift(l_ref, lp_ref, mul_ref)

    lo_wa = lo[:, 0:128]
    xw = w0_ref[...] + jnp.dot(jnp.tanh(lo_wa).astype(BF16), w2_ref[...], preferred_element_type=F32)
    nx = -xw
    softplus = jnp.maximum(nx, 0.0) + jnp.log1p(jnp.exp(-jnp.abs(nx)))
    w_log = -softplus - 0.5
    lw = -jnp.exp(w_log)
    tt = lw.shape[0]
    ci = lax.broadcasted_iota(jnp.int32, (tt, tt), 0)
    cj = lax.broadcasted_iota(jnp.int32, (tt, tt), 1)
    chunk_tri = ((ci // SCAN_CHUNK == cj // SCAN_CHUNK) & (cj <= ci)).astype(BF16)
    cs = sum(jnp.dot(chunk_tri, term, preferred_element_type=F32) for term in _bf16_terms(lw, 3))
    cso_ref[0] = cs
    a = _sigmoid(a0_ref[...] + jnp.dot(lo_wa.astype(BF16), a2_ref[...], preferred_element_type=F32))
    go_ref[0] = jnp.dot(_sigmoid(lo[:, 128:384]).astype(BF16), g2_ref[...], preferred_element_type=F32)

    kk = k * kk_ref[...]
    kk = kk / jnp.maximum(jnp.sqrt(_head_sum(kk * kk)), 1e-12)
    k2 = k * (1.0 + (a - 1.0) * ka_ref[...])
    ro_ref[0] = r
    ko_ref[0] = k2
    vo_ref[0] = v
    ao_ref[0] = -kk * jnp.exp(cs - lw)
    bo_ref[0] = kk * a
    bonus_ref[0] = _head_sum(r * k2 * rk_ref[...]) * v


def _rwprep(p3, mu, w0, w2p, a0, a2p, g2p, kkp, kap, rkp, tt=256):
    b, t, _ = p3.shape
    nt = t // tt
    blk8 = tt // 8
    col = lambda c: (lambda i, j: (i, j, c))
    prev = lambda c: (lambda i, j: (i, jnp.maximum(j * blk8 - 1, 0), c))
    vec = lambda n: pl.BlockSpec((1, n), lambda i, j: (0, 0))
    lora_col = (POOL_DIM + 3 * RW_DIM) // LORA_PAD
    in_specs = [
        pl.BlockSpec((1, tt, RW_DIM), col(1)),
        pl.BlockSpec((1, tt, RW_DIM), col(2)),
        pl.BlockSpec((1, tt, RW_DIM), col(3)),
        pl.BlockSpec((1, tt, LORA_PAD), col(lora_col)),
        pl.BlockSpec((1, 8, RW_DIM), prev(1)),
        pl.BlockSpec((1, 8, RW_DIM), prev(2)),
        pl.BlockSpec((1, 8, RW_DIM), prev(3)),
        pl.BlockSpec((1, 8, LORA_PAD), prev(lora_col)),
        vec(RW_DIM), vec(RW_DIM), vec(RW_DIM), vec(LORA_PAD),
        vec(RW_DIM),
        pl.BlockSpec((128, RW_DIM), lambda i, j: (0, 0)),
        vec(RW_DIM),
        pl.BlockSpec((128, RW_DIM), lambda i, j: (0, 0)),
        pl.BlockSpec((256, RW_DIM), lambda i, j: (0, 0)),
        vec(RW_DIM), vec(RW_DIM), vec(RW_DIM),
    ]
    out_spec = pl.BlockSpec((1, tt, RW_DIM), lambda i, j: (i, j, 0))
    out_sds = jax.ShapeDtypeStruct((b, t, RW_DIM), F32)
    return pl.pallas_call(
        _rwprep_kernel,
        grid=(b, nt),
        in_specs=in_specs,
        out_specs=[out_spec] * 8,
        out_shape=[out_sds] * 8,
        compiler_params=_cparams(("parallel", "parallel")),
        name="rwprep",
    )(p3, p3, p3, p3, p3, p3, p3, p3, mu[0], mu[1], mu[2], mu[3], w0, w2p, a0, a2p, g2p, kkp, kap, rkp)


def _scan_kernel(r_ref, cs_ref, k_ref, v_ref, a_ref, b_ref, y_ref, h_ref):
    c_len, lanes = SCAN_CHUNK, SCAN_LANES

    @pl.when(pl.program_id(1) == 0)
    def _():
        h_ref[...] = jnp.zeros_like(h_ref)

    bmask_f = _head_mask(lanes)
    bmask = bmask_f.astype(BF16)
    bmask2 = jnp.concatenate([bmask, bmask], axis=1)
    row = lax.broadcasted_iota(jnp.int32, (c_len, lanes), 0)
    src = lax.broadcasted_iota(jnp.int32, (c_len, lanes), 1) % c_len
    strict, incl = src < row, src <= row
    ident = jnp.where(src == row, 1.0, 0.0)
    same = lambda n: (row // n) == (src // n)
    eye = (lax.broadcasted_iota(jnp.int32, (lanes, lanes), 0)
           == lax.broadcasted_iota(jnp.int32, (lanes, lanes), 1))

    def mm(x, y):
        return jnp.dot(x.astype(BF16), y.astype(BF16), preferred_element_type=F32)

    def stack(x):
        xb = jnp.concatenate([x.astype(BF16)] * SCAN_HEADS, axis=0)
        return xb * (bmask if x.shape[1] == lanes else bmask2)

    def apply(p, x):
        return jnp.dot(p.astype(BF16), stack(x), preferred_element_type=F32)

    def scores(x, y):
        return lax.dot_general(x.astype(BF16), stack(y), (((1,), (1,)), ((), ())), preferred_element_type=F32)

    n_grp = RW_DIM // lanes

    def chain(bi, g):
        sl = slice(g * lanes, (g + 1) * lanes)
        r, cs, k, v, a_t, b = (ref[bi, :, sl] for ref in (r_ref, cs_ref, k_ref, v_ref, a_ref, b_ref))
        tot = cs[c_len - 1:c_len, :]
        e_neg, e_rem = jnp.exp(-cs), jnp.exp(tot - cs)
        r_t = r * jnp.exp(cs)
        b_t, k_t = b * e_neg, k * e_neg
        b_h, k_h = b * e_rem, k * e_rem

        ar = jnp.concatenate([a_t, r_t], axis=0)
        sb, sk = scores(ar, b_t), scores(ar, k_t)
        p_ab = jnp.where(strict, sb[:c_len], 0.0)
        p_ak = jnp.where(strict, sk[:c_len], 0.0)
        p_rb = jnp.where(incl, sb[c_len:], 0.0)
        p_rk = jnp.where(incl, sk[c_len:], 0.0)
        yield

        a8 = jnp.where(same(8), p_ab, 0.0)
        e1 = ident + a8
        q2 = apply(a8, a8)
        x1 = apply(p_ak, v)
        y2 = apply(p_rk, v)
        yield
        tq = apply(q2, jnp.concatenate([e1, q2], axis=1))
        t1 = e1 + tq[:, :lanes]
        yield
        t_inv = t1 + apply(tq[:, lanes:], t1)
        yield
        for n in (16, 32, 64):
            off = jnp.where(same(n) & jnp.logical_not(same(n // 2)), p_ab, 0.0)
            tmp = apply(off, t_inv)
            yield
            t_inv = t_inv + apply(t_inv, tmp)
            yield

        av = apply(t_inv, jnp.concatenate([a_t, x1], axis=1))
        yield
        ry = apply(p_rb, av)
        r2 = r_t + ry[:, :lanes]
        y2 = y2 + ry[:, lanes:]
        mn = mm(b_h.T, av)
        m_c = jnp.where(bmask_f, mn[:, :lanes], 0.0) + jnp.where(eye, jnp.exp(tot), 0.0)
        n_c = jnp.where(bmask_f, mn[:, lanes:] + mm(k_h.T, v), 0.0)
        yield

        h0 = h_ref[bi * n_grp + g]
        y_ref[bi, :, sl] = mm(r2, h0) + y2
        h_ref[bi * n_grp + g] = mm(m_c, h0) + n_c

    chains = [chain(bi, g) for bi in range(r_ref.shape[0]) for g in range(n_grp)]
    while all([next(c, False) is None for c in chains]):
        pass


def _scan(r, cs, k, v, a, b, nb=2):
    bsz, t, _ = r.shape
    spec = pl.BlockSpec((nb, SCAN_CHUNK, RW_DIM), lambda i, c: (i, c, 0))
    return pl.pallas_call(
        _scan_kernel,
        grid=(bsz // nb, t // SCAN_CHUNK),
        in_specs=[spec] * 6,
        out_specs=spec,
        out_shape=jax.ShapeDtypeStruct((bsz, t, RW_DIM), F32),
        scratch_shapes=[pltpu.VMEM((nb * (RW_DIM // SCAN_LANES), SCAN_LANES, SCAN_LANES), F32)],
        compiler_params=_cparams(("parallel", "arbitrary")),
        name="wkv7_scan",
    )(r, cs, k, v, a, b)


def _rwpost_kernel(y_ref, bonus_ref, g_ref, lng_ref, lnb_ref, o_ref):
    y = y_ref[...]
    ym = _head_sum(y) * (1.0 / RW_HEAD)
    yc = y - ym
    yv = _head_sum(yc * yc) * (1.0 / RW_HEAD)
    out = yc * lax.rsqrt(yv + RW_GN_EPS) * lng_ref[...] + lnb_ref[...] + bonus_ref[...]
    o_ref[...] = (out * g_ref[...]).astype(BF16)


def _rwpost(y, bonus, g, ln_g, ln_b, tt=512):
    m = y.shape[0]
    spec = pl.BlockSpec((tt, RW_DIM), lambda i: (i, 0))
    vec = pl.BlockSpec((1, RW_DIM), lambda i: (0, 0))
    return pl.pallas_call(
        _rwpost_kernel,
        grid=(m // tt,),
        in_specs=[spec, spec, spec, vec, vec],
        out_specs=spec,
        out_shape=jax.ShapeDtypeStruct((m, RW_DIM), BF16),
        compiler_params=_cparams(("parallel",)),
        name="rwpost",
    )(y, bonus, g, ln_g, ln_b)


def _conv_kernel(a_ref, gate_ref, dw_ref, db_ref, lng_ref, lnb_ref, o_ref, buf_ref, y_ref):
    tt = a_ref.shape[1]
    rb, cb = 128, 128

    @pl.when(pl.program_id(1) == 0)
    def _():
        buf_ref[0:CV_HALO, :] = jnp.zeros((CV_HALO, CV_DIM), F32)
        buf_ref[CV_HALO + tt:, :] = jnp.zeros((CV_TAIL, CV_DIM), F32)

    buf_ref[CV_HALO:CV_HALO + tt, :] = a_ref[0] * _sigmoid(gate_ref[0])
    first = CV_HALO - (CV_WIDTH - 1)
    ext = rb + 16
    for r0 in range(0, tt, rb):
        for c0 in range(0, CV_DIM, cb):
            acc = jnp.broadcast_to(db_ref[:, c0:c0 + cb], (rb, cb))
            for s in range(8):
                zs = None
                for j in range(s, CV_WIDTH, 8):
                    term = dw_ref[j:j + 1, c0:c0 + cb] * buf_ref[r0 + j - s:r0 + j - s + ext, c0:c0 + cb]
                    zs = term if zs is None else zs + term
                acc = acc + zs[first + s:first + s + rb]
            y_ref[r0:r0 + rb, c0:c0 + cb] = acc
    buf_ref[0:CV_HALO, :] = buf_ref[tt:tt + CV_HALO, :]
    z = _layer_norm(y_ref[...], lng_ref[...], lnb_ref[...])
    o_ref[0] = (z * _sigmoid(z)).astype(BF16)


def _conv(p3, dw, db, ln_g, ln_b, tt=256):
    b, t, _ = p3.shape
    vec = pl.BlockSpec((1, CV_DIM), lambda i, j: (0, 0))
    return pl.pallas_call(
        _conv_kernel,
        grid=(b, t // tt),
        in_specs=[
            pl.BlockSpec((1, tt, CV_DIM), lambda i, j: (i, j, 0)),
            pl.BlockSpec((1, tt, CV_DIM), lambda i, j: (i, j, 1)),
            pl.BlockSpec((CV_HALO, CV_DIM), lambda i, j: (0, 0)),
            vec, vec, vec,
        ],
        out_specs=pl.BlockSpec((1, tt, CV_DIM), lambda i, j: (i, j, 0)),
        out_shape=jax.ShapeDtypeStruct((b, t, CV_DIM), BF16),
        scratch_shapes=[pltpu.VMEM((CV_HALO + tt + CV_TAIL, CV_DIM), F32), pltpu.VMEM((tt, CV_DIM), F32)],
        compiler_params=_cparams(("parallel", "arbitrary")),
        name="conv",
    )(p3, p3, dw, db, ln_g, ln_b)


def _gmlp_kernel(u_ref, v_ref, lng_ref, lnb_ref, ws_ref, bs_ref, o_ref):
    tt = u_ref.shape[1]
    n_chunks = tt // SG_CHUNK
    grp = SG_DIM // SG_GROUPS
    u = _gelu(u_ref[0])
    v = _gelu(v_ref[0])
    vb = _layer_norm(v, lng_ref[...], lnb_ref[...]).astype(BF16)
    tri = (lax.broadcasted_iota(jnp.int32, (SG_CHUNK, SG_CHUNK), 1)
           <= lax.broadcasted_iota(jnp.int32, (SG_CHUNK, SG_CHUNK), 0))
    for g in range(SG_GROUPS):
        lanes = slice(g * grp, (g + 1) * grp)
        wg = jnp.where(tri, ws_ref[g], 0.0).astype(BF16)
        vg = jnp.concatenate([vb[c * SG_CHUNK:(c + 1) * SG_CHUNK, lanes] for c in range(n_chunks)], axis=1)
        sg = jnp.dot(wg, vg, preferred_element_type=F32)
        bias = bs_ref[:, g:g + 1]
        for c in range(n_chunks):
            rows = slice(c * SG_CHUNK, (c + 1) * SG_CHUNK)
            s = sg[:, c * grp:(c + 1) * grp] + bias
            o_ref[0, rows, lanes] = (u[rows, lanes] * s).astype(BF16)


def _gmlp(p3, ln_g, ln_b, ws, bs_t, tt=512):
    b, t, _ = p3.shape
    vec = pl.BlockSpec((1, SG_DIM), lambda i, j: (0, 0))
    return pl.pallas_call(
        _gmlp_kernel,
        grid=(b, t // tt),
        in_specs=[
            pl.BlockSpec((1, tt, SG_DIM), lambda i, j: (i, j, 2)),
            pl.BlockSpec((1, tt, SG_DIM), lambda i, j: (i, j, 3)),
            vec, vec,
            pl.BlockSpec((SG_GROUPS, SG_CHUNK, SG_CHUNK), lambda i, j: (0, 0, 0)),
            pl.BlockSpec((SG_CHUNK, SG_GROUPS), lambda i, j: (0, 0)),
        ],
        out_specs=pl.BlockSpec((1, tt, SG_DIM), lambda i, j: (i, j, 0)),
        out_shape=jax.ShapeDtypeStruct((b, t, SG_DIM), BF16),
        compiler_params=_cparams(("parallel", "parallel")),
        name="gmlp",
    )(p3, p3, ln_g, ln_b, ws, bs_t)


def _pad_rows(w, start, rows):
    return jnp.zeros((rows, w.shape[1]), BF16).at[start:start + w.shape[0]].set(w.astype(BF16))


def kernel(x, norm_g, ffn_w1, ffn_w3, ffn_w2, ev_w_in, ev_mu, pool_w, pool_scale, rw_w0, rw_w2, rw_a0, rw_a2,
           rw_g2, rw_kk, rw_ka, rw_rk, rw_ln_g, rw_ln_b, ev_w_out, od_w_in, cv_dw, cv_db, cv_ln_g, cv_ln_b,
           sg_ln_g, sg_ln_b, sg_ws, sg_b, od_w_out):
    bsz, t, d = x.shape
    m = bsz * t
    depth = norm_g.shape[0]
    row = lambda vct: vct.reshape(1, -1)

    wb = tuple(w[0, 0].astype(BF16) for w in (ffn_w1, ffn_w3, ffn_w2))
    ev_w_in_b = jnp.pad(ev_w_in.astype(BF16), ((0, 0), (0, 0), (0, EV_IN_PAD - EV_IN)))
    ev_w_out_b, od_w_in_b, od_w_out_b = ev_w_out.astype(BF16), od_w_in.astype(BF16), od_w_out.astype(BF16)
    h = x.reshape(m, d)
    for layer in range(depth):
        g = norm_g[layer]
        h, wb = _ffn(h, row(g[0]), row(g[1]), wb, nxt=(ffn_w1, ffn_w3, ffn_w2, layer, 1))
        if layer % 2 == 0:
            e = layer // 2
            p3 = _inproj(h, row(g[2]), ev_w_in_b, e, tn=EV_IN_PAD // 3).reshape(bsz, t, EV_IN_PAD)
            m1 = _pool(p3, pool_w[e].astype(BF16), row(pool_scale[e])).reshape(m, POOL_DIM)
            mu = ev_mu[e]
            mus = (row(mu[0:RW_DIM]), row(mu[RW_DIM:2 * RW_DIM]), row(mu[2 * RW_DIM:3 * RW_DIM]),
                   row(jnp.pad(mu[3 * RW_DIM:], (0, LORA_PAD - (EV_IN - POOL_DIM - 3 * RW_DIM)))))
            r, lw, k, v, a, b, gate, bonus = _rwprep(
                p3, mus, row(rw_w0[e]), _pad_rows(rw_w2[e], 0, 128), row(rw_a0[e]),
                _pad_rows(rw_a2[e], RW_DECAY_LORA, 128), _pad_rows(rw_g2[e], 0, 256),
                row(rw_kk[e]), row(rw_ka[e]), row(rw_rk[e]))
            y = _scan(r, lw, k, v, a, b)
            m2 = _rwpost(y.reshape(m, RW_DIM), bonus.reshape(m, RW_DIM), gate.reshape(m, RW_DIM),
                         row(rw_ln_g[e]), row(rw_ln_b[e]))
            h = _outproj(m1, m2, ev_w_out_b, e, h, row(g[3]))
        else:
            o = layer // 2
            p3 = _inproj(h, row(g[2]), od_w_in_b, o, tn=OD_IN // 4).reshape(bsz, t, OD_IN)
            dw = jnp.pad(cv_dw[o], ((0, CV_HALO - CV_WIDTH), (0, 0)))
            m1 = _conv(p3, dw, row(cv_db[o]), row(cv_ln_g[o]), row(cv_ln_b[o])).reshape(m, CV_DIM)
            m2 = _gmlp(p3, row(sg_ln_g[o]), row(sg_ln_b[o]), sg_ws[o], sg_b[o].T).reshape(m, SG_DIM)
            h = _outproj(m1, m2, od_w_out_b, o, h, row(g[3]))
        nxt = (ffn_w1, ffn_w3, ffn_w2, layer + 1, 0) if layer + 1 < depth else None
        h, wb = _ffn(h, row(g[4]), row(g[5]), wb, nxt=nxt)
    return h.reshape(bsz, t, d)
```

```python
import functools

import jax
import jax.numpy as jnp
from jax import lax
from jax.experimental import pallas as pl
from jax.experimental.pallas import tpu as pltpu

F32 = jnp.float32
BF16 = jnp.bfloat16

D_MODEL = 2048
D_FF = 5632
NORM_EPS = 1e-6
LN_EPS = 1e-5
POOL_DIM = 1024
POOL_WINDOWS = (2, 4, 8, 16)
POOL_GROUP_DIM = 256
RW_DIM = 1024
RW_HEAD = 64
RW_DECAY_LORA = 64
RW_A_LORA = 64
RW_GATE_LORA = 160
RW_GN_EPS = 64e-5
EV_IN = POOL_DIM + 3 * RW_DIM + RW_DECAY_LORA + RW_A_LORA + RW_GATE_LORA
LORA_PAD = 512
EV_IN_PAD = POOL_DIM + 3 * RW_DIM + LORA_PAD
CV_DIM = 1024
CV_WIDTH = 31
CV_HALO = 32
CV_TAIL = 8
SG_DIM = 1024
SG_CHUNK = 128
SG_GROUPS = 8
OD_IN = 2 * CV_DIM + 2 * SG_DIM

SCAN_CHUNK = 64
SCAN_LANES = 256
SCAN_HEADS = SCAN_LANES // RW_HEAD

V7X_VMEM_BYTES = 64 * 1024 * 1024
VMEM_LIMIT = V7X_VMEM_BYTES - 4 * 1024 * 1024


def _cparams(sem):
    return pltpu.CompilerParams(dimension_semantics=sem, vmem_limit_bytes=VMEM_LIMIT)


def _rms(x, g):
    return x * lax.rsqrt(jnp.mean(x * x, axis=-1, keepdims=True) + NORM_EPS) * g


def _layer_norm(x, g, b):
    mu = jnp.mean(x, axis=-1, keepdims=True)
    xc = x - mu
    var = jnp.mean(xc * xc, axis=-1, keepdims=True)
    return xc * lax.rsqrt(var + LN_EPS) * g + b


def _sigmoid(x):
    return jax.nn.sigmoid(x)


def _gelu(x):
    return 0.5 * x * (1.0 + lax.erf(x * (2.0 ** -0.5)))


def _head_mask(n):
    r = lax.broadcasted_iota(jnp.int32, (n, n), 0) // RW_HEAD
    c = lax.broadcasted_iota(jnp.int32, (n, n), 1) // RW_HEAD
    return r == c


def _bf16_terms(x, n):
    terms = []
    for _ in range(n - 1):
        t = x.astype(BF16)
        terms.append(t)
        x = x - t.astype(F32)
    terms.append(x.astype(BF16))
    return terms


def _head_sum(x):
    ones_bd = _head_mask(SCAN_LANES).astype(BF16)
    parts = []
    for i in range(0, x.shape[1], SCAN_LANES):
        hi, lo = _bf16_terms(x[:, i:i + SCAN_LANES], 2)
        parts.append(jnp.dot(hi, ones_bd, preferred_element_type=F32)
                     + jnp.dot(lo, ones_bd, preferred_element_type=F32))
    return jnp.concatenate(parts, axis=1)


def _ffn_kernel(*refs, cast_next, emit_next):
    refs = list(refs)
    h_ref, gpre_ref, gpost_ref = refs[:3]
    del refs[:3]
    gnext_ref = refs.pop(0) if emit_next else None
    w1_ref, w3_ref, w2_ref = refs[:3]
    del refs[:3]
    f32_refs = tuple(refs[:3]) if cast_next else ()
    del refs[:len(f32_refs)]
    o_ref = refs.pop(0)
    xo_ref = refs.pop(0) if emit_next else None
    bf16_refs = tuple(refs[:3]) if cast_next else ()
    xn_ref = refs[-1]
    j = pl.program_id(1)

    @pl.when(j == 0)
    def _():
        xn_ref[...] = _rms(h_ref[...], gpre_ref[...]).astype(BF16)
        o_ref[...] = jnp.zeros_like(o_ref)

    for src_ref, dst_ref in zip(f32_refs, bf16_refs, strict=True):
        dst_ref[...] = src_ref[...].astype(BF16)
    x = xn_ref[...]
    a = jnp.dot(x, w1_ref[...], preferred_element_type=F32)
    b = jnp.dot(x, w3_ref[...], preferred_element_type=F32)
    hid = (a * _sigmoid(a) * b).astype(BF16)
    o_ref[...] += jnp.dot(hid, w2_ref[...], preferred_element_type=F32)

    @pl.when(j == pl.num_programs(1) - 1)
    def _():
        res = h_ref[...] + 0.5 * _rms(o_ref[...], gpost_ref[...])
        o_ref[...] = res
        if emit_next:
            xo_ref[...] = _rms(res, gnext_ref[...]).astype(BF16)


def _ffn(h, gpre, gpost, wb, nxt=None, gnext=None, tm=1024, tf=256):
    m = h.shape[0]
    n_i, n_j = m // tm, D_FF // tf
    vec = pl.BlockSpec((1, D_MODEL), lambda i, j: (0, 0))
    tile = pl.BlockSpec((tm, D_MODEL), lambda i, j: (i, 0))
    in_specs = [tile, vec, vec] + ([vec] if gnext is not None else []) + [
        pl.BlockSpec((D_MODEL, tf), lambda i, j: (0, j)),
        pl.BlockSpec((D_MODEL, tf), lambda i, j: (0, j)),
        pl.BlockSpec((tf, D_MODEL), lambda i, j: (j, 0)),
    ]
    out_specs = [tile]
    out_shape = [jax.ShapeDtypeStruct((m, D_MODEL), F32)]
    args = [h, gpre, gpost] + ([gnext] if gnext is not None else []) + list(wb)
    if gnext is not None:
        out_specs.append(tile)
        out_shape.append(jax.ShapeDtypeStruct((m, D_MODEL), BF16))
    if nxt is not None:
        f1, f3, f2, layer, idx = nxt
        dm, df = D_MODEL // n_i, D_FF // n_j
        in_specs += [
            pl.BlockSpec((None, None, dm, df), lambda i, j: (layer, idx, i, j)),
            pl.BlockSpec((None, None, dm, df), lambda i, j: (layer, idx, i, j)),
            pl.BlockSpec((None, None, df, dm), lambda i, j: (layer, idx, j, i)),
        ]
        out_specs += [
            pl.BlockSpec((dm, df), lambda i, j: (i, j)),
            pl.BlockSpec((dm, df), lambda i, j: (i, j)),
            pl.BlockSpec((df, dm), lambda i, j: (j, i)),
        ]
        out_shape += [jax.ShapeDtypeStruct((D_MODEL, D_FF), BF16), jax.ShapeDtypeStruct((D_MODEL, D_FF), BF16),
                      jax.ShapeDtypeStruct((D_FF, D_MODEL), BF16)]
        args += [f1, f3, f2]
    outs = pl.pallas_call(
        functools.partial(_ffn_kernel, cast_next=nxt is not None, emit_next=gnext is not None),
        grid=(n_i, n_j),
        in_specs=in_specs,
        out_specs=out_specs,
        out_shape=out_shape,
        scratch_shapes=[pltpu.VMEM((tm, D_MODEL), BF16)],
        compiler_params=_cparams(("parallel", "arbitrary")),
        name="ffn",
    )(*args)
    outs = list(outs)
    h_new = outs.pop(0)
    xn = outs.pop(0) if gnext is not None else None
    return h_new, xn, (tuple(outs) if nxt is not None else None)


def _inproj_kernel(x_ref, w_ref, o_ref):
    o_ref[...] = jnp.dot(x_ref[...], w_ref[...], preferred_element_type=F32)


def _inproj(xn, w, e, tn, tm=1024):
    m = xn.shape[0]
    n = w.shape[2]
    return pl.pallas_call(
        _inproj_kernel,
        grid=(m // tm, n // tn),
        in_specs=[
            pl.BlockSpec((tm, D_MODEL), lambda i, j: (i, 0)),
            pl.BlockSpec((None, D_MODEL, tn), lambda i, j: (e, 0, j)),
        ],
        out_specs=pl.BlockSpec((tm, tn), lambda i, j: (i, j)),
        out_shape=jax.ShapeDtypeStruct((m, n), F32),
        compiler_params=_cparams(("parallel", "parallel")),
        name="inproj",
    )(xn, w)


def _outproj_kernel(x1_ref, x2_ref, w_ref, h_ref, g_ref, o_ref):
    n1 = x1_ref.shape[1]
    m = jnp.dot(x1_ref[...], w_ref[0:n1, :], preferred_element_type=F32)
    m = m + jnp.dot(x2_ref[...], w_ref[n1:, :], preferred_element_type=F32)
    o_ref[...] = h_ref[...] + _rms(m, g_ref[...])


def _outproj(x1, x2, w, e, h, g, tm=512):
    m = h.shape[0]
    n1, n2 = x1.shape[1], x2.shape[1]
    return pl.pallas_call(
        _outproj_kernel,
        grid=(m // tm,),
        in_specs=[
            pl.BlockSpec((tm, n1), lambda i: (i, 0)),
            pl.BlockSpec((tm, n2), lambda i: (i, 0)),
            pl.BlockSpec((None, n1 + n2, D_MODEL), lambda i: (e, 0, 0)),
            pl.BlockSpec((tm, D_MODEL), lambda i: (i, 0)),
            pl.BlockSpec((1, D_MODEL), lambda i: (0, 0)),
        ],
        out_specs=pl.BlockSpec((tm, D_MODEL), lambda i: (i, 0)),
        out_shape=jax.ShapeDtypeStruct((m, D_MODEL), F32),
        compiler_params=_cparams(("parallel",)),
        name="outproj",
    )(x1, x2, w, h, g)


def _shift_rows(x, k):
    row = lax.broadcasted_iota(jnp.int32, x.shape, 0)
    return jnp.where(row >= k, pltpu.roll(x, k, axis=0), 0.0)


def _pool_kernel(p_ref, w_ref, s_ref, o_ref):
    gi = pl.program_id(1)
    x = p_ref[0]
    s2 = x + _shift_rows(x, 1)
    s4 = s2 + _shift_rows(s2, 2)
    s8 = s4 + _shift_rows(s4, 4)
    s16 = s8 + _shift_rows(s8, 8)
    wsum = jnp.where(gi == 0, s2, jnp.where(gi == 1, s4, jnp.where(gi == 2, s8, s16)))
    win = jnp.left_shift(2, gi)
    t = lax.broadcasted_iota(jnp.int32, (x.shape[0], 1), 0)
    cnt = jnp.minimum(t + 1, win).astype(F32)
    d = (wsum / cnt - x).astype(BF16)
    o_ref[0] = (jnp.dot(d, w_ref[0], preferred_element_type=F32) * s_ref[...]).astype(BF16)


def _pool(p3, pool_w, pool_scale):
    b, t, _ = p3.shape
    n_grp = len(POOL_WINDOWS)
    return pl.pallas_call(
        _pool_kernel,
        grid=(b, n_grp),
        in_specs=[
            pl.BlockSpec((1, t, POOL_GROUP_DIM), lambda i, g: (i, 0, g)),
            pl.BlockSpec((1, POOL_GROUP_DIM, POOL_GROUP_DIM), lambda i, g: (g, 0, 0)),
            pl.BlockSpec((1, POOL_GROUP_DIM), lambda i, g: (0, g)),
        ],
        out_specs=pl.BlockSpec((1, t, POOL_GROUP_DIM), lambda i, g: (i, 0, g)),
        out_shape=jax.ShapeDtypeStruct((b, t, POOL_DIM), BF16),
        compiler_params=_cparams(("parallel", "parallel")),
        name="pool",
    )(p3, pool_w, pool_scale)


def _rwprep_kernel(r_ref, k_ref, v_ref, l_ref, rp_ref, kp_ref, vp_ref, lp_ref,
                   mur_ref, muk_ref, muv_ref, mul_ref, w0_ref, w2_ref, a0_ref, a2_ref, g2_ref,
                   kk_ref, ka_ref, rk_ref,
                   ro_ref, cso_ref, ko_ref, vo_ref, ao_ref, bo_ref, go_ref, bonus_ref):
    ti = pl.program_id(1)

    def token_shift(cur_ref, prev_ref, mu_ref):
        x = cur_ref[0]
        prev_row = jnp.where(ti > 0, prev_ref[0, 7:8, :], 0.0)
        row = lax.broadcasted_iota(jnp.int32, x.shape, 0)
        xs = jnp.where(row == 0, prev_row, pltpu.roll(x, 1, axis=0))
        return x + (xs - x) * mu_ref[...]

    r = token_shift(r_ref, rp_ref, mur_ref)
    k = token_shift(k_ref, kp_ref, muk_ref)
    v = token_shift(v_ref, vp_ref, muv_ref)
    lo = token_shift(l_ref, lp_ref, mul_ref)

    lo_wa = lo[:, 0:128]
    xw = w0_ref[...] + jnp.dot(jnp.tanh(lo_wa).astype(BF16), w2_ref[...], preferred_element_type=F32)
    nx = -xw
    softplus = jnp.maximum(nx, 0.0) + jnp.log1p(jnp.exp(-jnp.abs(nx)))
    w_log = -softplus - 0.5
    lw = -jnp.exp(w_log)
    tt = lw.shape[0]
    ci = lax.broadcasted_iota(jnp.int32, (tt, tt), 0)
    cj = lax.broadcasted_iota(jnp.int32, (tt, tt), 1)
    chunk_tri = ((ci // SCAN_CHUNK == cj // SCAN_CHUNK) & (cj <= ci)).astype(BF16)
    cs = sum(jnp.dot(chunk_tri, term, preferred_element_type=F32) for term in _bf16_terms(lw, 3))
    cso_ref[0] = cs
    a = _sigmoid(a0_ref[...] + jnp.dot(lo_wa.astype(BF16), a2_ref[...], preferred_element_type=F32))
    go_ref[0] = jnp.dot(_sigmoid(lo[:, 128:384]).astype(BF16), g2_ref[...], preferred_element_type=F32)

    kk = k * kk_ref[...]
    kk = kk / jnp.maximum(jnp.sqrt(_head_sum(kk * kk)), 1e-12)
    k2 = k * (1.0 + (a - 1.0) * ka_ref[...])
    ro_ref[0] = r
    ko_ref[0] = k2
    vo_ref[0] = v
    ao_ref[0] = -kk * jnp.exp(cs - lw)
    bo_ref[0] = kk * a
    bonus_ref[0] = _head_sum(r * k2 * rk_ref[...]) * v


def _rwprep(p3, mu, w0, w2p, a0, a2p, g2p, kkp, kap, rkp, tt=256):
    b, t, _ = p3.shape
    nt = t // tt
    blk8 = tt // 8
    col = lambda c: (lambda i, j: (i, j, c))
    prev = lambda c: (lambda i, j: (i, jnp.maximum(j * blk8 - 1, 0), c))
    vec = lambda n: pl.BlockSpec((1, n), lambda i, j: (0, 0))
    lora_col = (POOL_DIM + 3 * RW_DIM) // LORA_PAD
    in_specs = [
        pl.BlockSpec((1, tt, RW_DIM), col(1)),
        pl.BlockSpec((1, tt, RW_DIM), col(2)),
        pl.BlockSpec((1, tt, RW_DIM), col(3)),
        pl.BlockSpec((1, tt, LORA_PAD), col(lora_col)),
        pl.BlockSpec((1, 8, RW_DIM), prev(1)),
        pl.BlockSpec((1, 8, RW_DIM), prev(2)),
        pl.BlockSpec((1, 8, RW_DIM), prev(3)),
        pl.BlockSpec((1, 8, LORA_PAD), prev(lora_col)),
        vec(RW_DIM), vec(RW_DIM), vec(RW_DIM), vec(LORA_PAD),
        vec(RW_DIM),
        pl.BlockSpec((128, RW_DIM), lambda i, j: (0, 0)),
        vec(RW_DIM),
        pl.BlockSpec((128, RW_DIM), lambda i, j: (0, 0)),
        pl.BlockSpec((256, RW_DIM), lambda i, j: (0, 0)),
        vec(RW_DIM), vec(RW_DIM), vec(RW_DIM),
    ]
    out_spec = pl.BlockSpec((1, tt, RW_DIM), lambda i, j: (i, j, 0))
    out_sds = jax.ShapeDtypeStruct((b, t, RW_DIM), F32)
    return pl.pallas_call(
        _rwprep_kernel,
        grid=(b, nt),
        in_specs=in_specs,
        out_specs=[out_spec] * 8,
        out_shape=[out_sds] * 8,
        compiler_params=_cparams(("parallel", "parallel")),
        name="rwprep",
    )(p3, p3, p3, p3, p3, p3, p3, p3, mu[0], mu[1], mu[2], mu[3], w0, w2p, a0, a2p, g2p, kkp, kap, rkp)


def _scan_kernel(r_ref, cs_ref, k_ref, v_ref, a_ref, b_ref, y_ref, h_ref):
    c_len, lanes = SCAN_CHUNK, SCAN_LANES

    @pl.when(pl.program_id(1) == 0)
    def _():
        h_ref[...] = jnp.zeros_like(h_ref)

    bmask_f = _head_mask(lanes)
    bmask = bmask_f.astype(BF16)
    bmask2 = jnp.concatenate([bmask, bmask], axis=1)
    row = lax.broadcasted_iota(jnp.int32, (c_len, lanes), 0)
    src = lax.broadcasted_iota(jnp.int32, (c_len, lanes), 1) % c_len
    strict, incl = src < row, src <= row
    ident = jnp.where(src == row, 1.0, 0.0)
    same = lambda n: (row // n) == (src // n)
    eye = (lax.broadcasted_iota(jnp.int32, (lanes, lanes), 0)
           == lax.broadcasted_iota(jnp.int32, (lanes, lanes), 1))

    def mm(x, y):
        return jnp.dot(x.astype(BF16), y.astype(BF16), preferred_element_type=F32)

    def stack(x):
        xb = jnp.concatenate([x.astype(BF16)] * SCAN_HEADS, axis=0)
        return xb * (bmask if x.shape[1] == lanes else bmask2)

    def apply(p, x):
        return jnp.dot(p.astype(BF16), stack(x), preferred_element_type=F32)

    def scores(x, y):
        return lax.dot_general(x.astype(BF16), stack(y), (((1,), (1,)), ((), ())), preferred_element_type=F32)

    n_grp = RW_DIM // lanes

    def chain(bi, g):
        sl = slice(g * lanes, (g + 1) * lanes)
        r, cs, k, v, a_t, b = (ref[bi, :, sl] for ref in (r_ref, cs_ref, k_ref, v_ref, a_ref, b_ref))
        tot = cs[c_len - 1:c_len, :]
        e_neg, e_rem = jnp.exp(-cs), jnp.exp(tot - cs)
        r_t = r * jnp.exp(cs)
        b_t, k_t = b * e_neg, k * e_neg
        b_h, k_h = b * e_rem, k * e_rem

        ar = jnp.concatenate([a_t, r_t], axis=0)
        sb, sk = scores(ar, b_t), scores(ar, k_t)
        p_ab = jnp.where(strict, sb[:c_len], 0.0)
        p_ak = jnp.where(strict, sk[:c_len], 0.0)
        p_rb = jnp.where(incl, sb[c_len:], 0.0)
        p_rk = jnp.where(incl, sk[c_len:], 0.0)
        yield

        a8 = jnp.where(same(8), p_ab, 0.0)
        e1 = ident + a8
        q2 = apply(a8, a8)
        xy = apply(jnp.concatenate([p_ak, p_rk], axis=0), v)
        x1, y2 = xy[:c_len], xy[c_len:]
        yield
        tq = apply(q2, jnp.concatenate([e1, q2], axis=1))
        t1 = e1 + tq[:, :lanes]
        yield
        t_inv = t1 + apply(tq[:, lanes:], t1)
        yield
        for n in (16, 32, 64):
            off = jnp.where(same(n) & jnp.logical_not(same(n // 2)), p_ab, 0.0)
            tmp = apply(off, t_inv)
            yield
            t_inv = t_inv + apply(t_inv, tmp)
            yield

        av = apply(t_inv, jnp.concatenate([a_t, x1], axis=1))
        yield
        ry = apply(p_rb, av)
        r2 = r_t + ry[:, :lanes]
        y2 = y2 + ry[:, lanes:]
        mn = mm(b_h.T, av)
        m_c = jnp.where(bmask_f, mn[:, :lanes], 0.0) + jnp.where(eye, jnp.exp(tot), 0.0)
        n_c = jnp.where(bmask_f, mn[:, lanes:] + mm(k_h.T, v), 0.0)
        yield

        h0 = h_ref[bi * n_grp + g]
        y_ref[bi, :, sl] = mm(r2, h0) + y2
        h_ref[bi * n_grp + g] = mm(m_c, h0) + n_c

    chains = [chain(bi, g) for bi in range(r_ref.shape[0]) for g in range(n_grp)]
    while all([next(c, False) is None for c in chains]):
        pass


def _scan(r, cs, k, v, a, b, nb=2):
    bsz, t, _ = r.shape
    spec = pl.BlockSpec((nb, SCAN_CHUNK, RW_DIM), lambda i, c: (i, c, 0))
    return pl.pallas_call(
        _scan_kernel,
        grid=(bsz // nb, t // SCAN_CHUNK),
        in_specs=[spec] * 6,
        out_specs=spec,
        out_shape=jax.ShapeDtypeStruct((bsz, t, RW_DIM), F32),
        scratch_shapes=[pltpu.VMEM((nb * (RW_DIM // SCAN_LANES), SCAN_LANES, SCAN_LANES), F32)],
        compiler_params=_cparams(("parallel", "arbitrary")),
        name="wkv7_scan",
    )(r, cs, k, v, a, b)


def _rwpost_kernel(y_ref, bonus_ref, g_ref, lng_ref, lnb_ref, o_ref):
    y = y_ref[...]
    ym = _head_sum(y) * (1.0 / RW_HEAD)
    yc = y - ym
    yv = _head_sum(yc * yc) * (1.0 / RW_HEAD)
    out = yc * lax.rsqrt(yv + RW_GN_EPS) * lng_ref[...] + lnb_ref[...] + bonus_ref[...]
    o_ref[...] = (out * g_ref[...]).astype(BF16)


def _rwpost(y, bonus, g, ln_g, ln_b, tt=512):
    m = y.shape[0]
    spec = pl.BlockSpec((tt, RW_DIM), lambda i: (i, 0))
    vec = pl.BlockSpec((1, RW_DIM), lambda i: (0, 0))
    return pl.pallas_call(
        _rwpost_kernel,
        grid=(m // tt,),
        in_specs=[spec, spec, spec, vec, vec],
        out_specs=spec,
        out_shape=jax.ShapeDtypeStruct((m, RW_DIM), BF16),
        compiler_params=_cparams(("parallel",)),
        name="rwpost",
    )(y, bonus, g, ln_g, ln_b)


def _conv_kernel(a_ref, gate_ref, dw_ref, db_ref, lng_ref, lnb_ref, o_ref, buf_ref, y_ref):
    tt = a_ref.shape[1]
    rb, cb = 128, 128

    @pl.when(pl.program_id(1) == 0)
    def _():
        buf_ref[0:CV_HALO, :] = jnp.zeros((CV_HALO, CV_DIM), F32)
        buf_ref[CV_HALO + tt:, :] = jnp.zeros((CV_TAIL, CV_DIM), F32)

    buf_ref[CV_HALO:CV_HALO + tt, :] = a_ref[0] * _sigmoid(gate_ref[0])
    first = CV_HALO - (CV_WIDTH - 1)
    ext = rb + 16
    for r0 in range(0, tt, rb):
        for c0 in range(0, CV_DIM, cb):
            acc = jnp.broadcast_to(db_ref[:, c0:c0 + cb], (rb, cb))
            for s in range(8):
                zs = None
                for j in range(s, CV_WIDTH, 8):
                    term = dw_ref[j:j + 1, c0:c0 + cb] * buf_ref[r0 + j - s:r0 + j - s + ext, c0:c0 + cb]
                    zs = term if zs is None else zs + term
                acc = acc + zs[first + s:first + s + rb]
            y_ref[r0:r0 + rb, c0:c0 + cb] = acc
    buf_ref[0:CV_HALO, :] = buf_ref[tt:tt + CV_HALO, :]
    z = _layer_norm(y_ref[...], lng_ref[...], lnb_ref[...])
    o_ref[0] = (z * _sigmoid(z)).astype(BF16)


def _conv(p3, dw, db, ln_g, ln_b, tt=256):
    b, t, _ = p3.shape
    vec = pl.BlockSpec((1, CV_DIM), lambda i, j: (0, 0))
    return pl.pallas_call(
        _conv_kernel,
        grid=(b, t // tt),
        in_specs=[
            pl.BlockSpec((1, tt, CV_DIM), lambda i, j: (i, j, 0)),
            pl.BlockSpec((1, tt, CV_DIM), lambda i, j: (i, j, 1)),
            pl.BlockSpec((CV_HALO, CV_DIM), lambda i, j: (0, 0)),
            vec, vec, vec,
        ],
        out_specs=pl.BlockSpec((1, tt, CV_DIM), lambda i, j: (i, j, 0)),
        out_shape=jax.ShapeDtypeStruct((b, t, CV_DIM), BF16),
        scratch_shapes=[pltpu.VMEM((CV_HALO + tt + CV_TAIL, CV_DIM), F32), pltpu.VMEM((tt, CV_DIM), F32)],
        compiler_params=_cparams(("parallel", "arbitrary")),
        name="conv",
    )(p3, p3, dw, db, ln_g, ln_b)


def _gmlp_kernel(u_ref, v_ref, lng_ref, lnb_ref, ws_ref, bs_ref, o_ref):
    tt = u_ref.shape[1]
    n_chunks = tt // SG_CHUNK
    grp = SG_DIM // SG_GROUPS
    u = _gelu(u_ref[0])
    v = _gelu(v_ref[0])
    vb = _layer_norm(v, lng_ref[...], lnb_ref[...]).astype(BF16)
    tri = (lax.broadcasted_iota(jnp.int32, (SG_CHUNK, SG_CHUNK), 1)
           <= lax.broadcasted_iota(jnp.int32, (SG_CHUNK, SG_CHUNK), 0))
    for g in range(SG_GROUPS):
        lanes = slice(g * grp, (g + 1) * grp)
        wg = jnp.where(tri, ws_ref[g], 0.0).astype(BF16)
        vg = jnp.concatenate([vb[c * SG_CHUNK:(c + 1) * SG_CHUNK, lanes] for c in range(n_chunks)], axis=1)
        sg = jnp.dot(wg, vg, preferred_element_type=F32)
        bias = bs_ref[:, g:g + 1]
        for c in range(n_chunks):
            rows = slice(c * SG_CHUNK, (c + 1) * SG_CHUNK)
            s = sg[:, c * grp:(c + 1) * grp] + bias
            o_ref[0, rows, lanes] = (u[rows, lanes] * s).astype(BF16)


def _gmlp(p3, ln_g, ln_b, ws, bs_t, tt=512):
    b, t, _ = p3.shape
    vec = pl.BlockSpec((1, SG_DIM), lambda i, j: (0, 0))
    return pl.pallas_call(
        _gmlp_kernel,
        grid=(b, t // tt),
        in_specs=[
            pl.BlockSpec((1, tt, SG_DIM), lambda i, j: (i, j, 2)),
            pl.BlockSpec((1, tt, SG_DIM), lambda i, j: (i, j, 3)),
            vec, vec,
            pl.BlockSpec((SG_GROUPS, SG_CHUNK, SG_CHUNK), lambda i, j: (0, 0, 0)),
            pl.BlockSpec((SG_CHUNK, SG_GROUPS), lambda i, j: (0, 0)),
        ],
        out_specs=pl.BlockSpec((1, tt, SG_DIM), lambda i, j: (i, j, 0)),
        out_shape=jax.ShapeDtypeStruct((b, t, SG_DIM), BF16),
        compiler_params=_cparams(("parallel", "parallel")),
        name="gmlp",
    )(p3, p3, ln_g, ln_b, ws, bs_t)


def _pad_rows(w, start, rows):
    return jnp.zeros((rows, w.shape[1]), BF16).at[start:start + w.shape[0]].set(w.astype(BF16))


def kernel(x, norm_g, ffn_w1, ffn_w3, ffn_w2, ev_w_in, ev_mu, pool_w, pool_scale, rw_w0, rw_w2, rw_a0, rw_a2,
           rw_g2, rw_kk, rw_ka, rw_rk, rw_ln_g, rw_ln_b, ev_w_out, od_w_in, cv_dw, cv_db, cv_ln_g, cv_ln_b,
           sg_ln_g, sg_ln_b, sg_ws, sg_b, od_w_out):
    bsz, t, d = x.shape
    m = bsz * t
    depth = norm_g.shape[0]
    row = lambda vct: vct.reshape(1, -1)

    wb = tuple(w[0, 0].astype(BF16) for w in (ffn_w1, ffn_w3, ffn_w2))
    ev_w_in_b = jnp.pad(ev_w_in.astype(BF16), ((0, 0), (0, 0), (0, EV_IN_PAD - EV_IN)))
    ev_w_out_b, od_w_in_b, od_w_out_b = ev_w_out.astype(BF16), od_w_in.astype(BF16), od_w_out.astype(BF16)
    h = x.reshape(m, d)
    for layer in range(depth):
        g = norm_g[layer]
        h, xn, wb = _ffn(h, row(g[0]), row(g[1]), wb, nxt=(ffn_w1, ffn_w3, ffn_w2, layer, 1), gnext=row(g[2]),
                         tm=512, tf=512)
        if layer % 2 == 0:
            e = layer // 2
            p3 = _inproj(xn, ev_w_in_b, e, tn=EV_IN_PAD // 3).reshape(bsz, t, EV_IN_PAD)
            m1 = _pool(p3, pool_w[e].astype(BF16), row(pool_scale[e])).reshape(m, POOL_DIM)
            mu = ev_mu[e]
            mus = (row(mu[0:RW_DIM]), row(mu[RW_DIM:2 * RW_DIM]), row(mu[2 * RW_DIM:3 * RW_DIM]),
                   row(jnp.pad(mu[3 * RW_DIM:], (0, LORA_PAD - (EV_IN - POOL_DIM - 3 * RW_DIM)))))
            r, lw, k, v, a, b, gate, bonus = _rwprep(
                p3, mus, row(rw_w0[e]), _pad_rows(rw_w2[e], 0, 128), row(rw_a0[e]),
                _pad_rows(rw_a2[e], RW_DECAY_LORA, 128), _pad_rows(rw_g2[e], 0, 256),
                row(rw_kk[e]), row(rw_ka[e]), row(rw_rk[e]))
            y = _scan(r, lw, k, v, a, b)
            m2 = _rwpost(y.reshape(m, RW_DIM), bonus.reshape(m, RW_DIM), gate.reshape(m, RW_DIM),
                         row(rw_ln_g[e]), row(rw_ln_b[e]))
            h = _outproj(m1, m2, ev_w_out_b, e, h, row(g[3]))
        else:
            o = layer // 2
            p3 = _inproj(xn, od_w_in_b, o, tn=OD_IN // 4).reshape(bsz, t, OD_IN)
            dw = jnp.pad(cv_dw[o], ((0, CV_HALO - CV_WIDTH), (0, 0)))
            m1 = _conv(p3, dw, row(cv_db[o]), row(cv_ln_g[o]), row(cv_ln_b[o])).reshape(m, CV_DIM)
            m2 = _gmlp(p3, row(sg_ln_g[o]), row(sg_ln_b[o]), sg_ws[o], sg_b[o].T).reshape(m, SG_DIM)
            h = _outproj(m1, m2, od_w_out_b, o, h, row(g[3]))
        nxt = (ffn_w1, ffn_w3, ffn_w2, layer + 1, 0) if layer + 1 < depth else None
        h, _, wb = _ffn(h, row(g[4]), row(g[5]), wb, nxt=nxt)
    return h.reshape(bsz, t, d)
```

```python
import functools

import jax
import jax.numpy as jnp
from jax import lax
from jax.experimental import pallas as pl
from jax.experimental.pallas import tpu as pltpu

F32 = jnp.float32
BF16 = jnp.bfloat16

D_MODEL = 2048
D_FF = 5632
NORM_EPS = 1e-6
LN_EPS = 1e-5
POOL_DIM = 1024
POOL_WINDOWS = (2, 4, 8, 16)
POOL_GROUP_DIM = 256
RW_DIM = 1024
RW_HEAD = 64
RW_DECAY_LORA = 64
RW_A_LORA = 64
RW_GATE_LORA = 160
RW_GN_EPS = 64e-5
EV_IN = POOL_DIM + 3 * RW_DIM + RW_DECAY_LORA + RW_A_LORA + RW_GATE_LORA
LORA_PAD = 512
EV_IN_PAD = POOL_DIM + 3 * RW_DIM + LORA_PAD
CV_DIM = 1024
CV_WIDTH = 31
CV_HALO = 32
CV_TAIL = 8
SG_DIM = 1024
SG_CHUNK = 128
SG_GROUPS = 8
OD_IN = 2 * CV_DIM + 2 * SG_DIM

SCAN_CHUNK = 64
SCAN_LANES = 256
SCAN_HEADS = SCAN_LANES // RW_HEAD

V7X_VMEM_BYTES = 64 * 1024 * 1024
VMEM_LIMIT = V7X_VMEM_BYTES - 4 * 1024 * 1024


def _cparams(sem):
    return pltpu.CompilerParams(dimension_semantics=sem, vmem_limit_bytes=VMEM_LIMIT)


def _rms(x, g):
    return x * lax.rsqrt(jnp.mean(x * x, axis=-1, keepdims=True) + NORM_EPS) * g


def _layer_norm(x, g, b):
    mu = jnp.mean(x, axis=-1, keepdims=True)
    xc = x - mu
    var = jnp.mean(xc * xc, axis=-1, keepdims=True)
    return xc * lax.rsqrt(var + LN_EPS) * g + b


def _sigmoid(x):
    return jax.nn.sigmoid(x)


def _gelu(x):
    return 0.5 * x * (1.0 + lax.erf(x * (2.0 ** -0.5)))


def _head_mask(n):
    r = lax.broadcasted_iota(jnp.int32, (n, n), 0) // RW_HEAD
    c = lax.broadcasted_iota(jnp.int32, (n, n), 1) // RW_HEAD
    return r == c


def _bf16_terms(x, n):
    terms = []
    for _ in range(n - 1):
        t = x.astype(BF16)
        terms.append(t)
        x = x - t.astype(F32)
    terms.append(x.astype(BF16))
    return terms


def _head_sum(x):
    ones_bd = _head_mask(SCAN_LANES).astype(BF16)
    parts = []
    for i in range(0, x.shape[1], SCAN_LANES):
        hi, lo = _bf16_terms(x[:, i:i + SCAN_LANES], 2)
        parts.append(jnp.dot(hi, ones_bd, preferred_element_type=F32)
                     + jnp.dot(lo, ones_bd, preferred_element_type=F32))
    return jnp.concatenate(parts, axis=1)


def _ffn_kernel(*refs, cast_next, emit_next):
    refs = list(refs)
    h_ref, gpre_ref, gpost_ref = refs[:3]
    del refs[:3]
    gnext_ref = refs.pop(0) if emit_next else None
    w1_ref, w3_ref, w2_ref = refs[:3]
    del refs[:3]
    f32_refs = tuple(refs[:3]) if cast_next else ()
    del refs[:len(f32_refs)]
    o_ref = refs.pop(0)
    xo_ref = refs.pop(0) if emit_next else None
    bf16_refs = tuple(refs[:3]) if cast_next else ()
    xn_ref = refs[-1]
    j = pl.program_id(1)

    @pl.when(j == 0)
    def _():
        xn_ref[...] = _rms(h_ref[...], gpre_ref[...]).astype(BF16)
        o_ref[...] = jnp.zeros_like(o_ref)

    for src_ref, dst_ref in zip(f32_refs, bf16_refs, strict=True):
        dst_ref[...] = src_ref[...].astype(BF16)
    x = xn_ref[...]
    a = jnp.dot(x, w1_ref[...], preferred_element_type=F32)
    b = jnp.dot(x, w3_ref[...], preferred_element_type=F32)
    hid = (a * _sigmoid(a) * b).astype(BF16)
    o_ref[...] += jnp.dot(hid, w2_ref[...], preferred_element_type=F32)

    @pl.when(j == pl.num_programs(1) - 1)
    def _():
        res = h_ref[...] + 0.5 * _rms(o_ref[...], gpost_ref[...])
        o_ref[...] = res
        if emit_next:
            xo_ref[...] = _rms(res, gnext_ref[...]).astype(BF16)


def _ffn(h, gpre, gpost, wb, nxt=None, gnext=None, tm=1024, tf=256):
    m = h.shape[0]
    n_i, n_j = m // tm, D_FF // tf
    vec = pl.BlockSpec((1, D_MODEL), lambda i, j: (0, 0))
    tile = pl.BlockSpec((tm, D_MODEL), lambda i, j: (i, 0))
    in_specs = [tile, vec, vec] + ([vec] if gnext is not None else []) + [
        pl.BlockSpec((D_MODEL, tf), lambda i, j: (0, j)),
        pl.BlockSpec((D_MODEL, tf), lambda i, j: (0, j)),
        pl.BlockSpec((tf, D_MODEL), lambda i, j: (j, 0)),
    ]
    out_specs = [tile]
    out_shape = [jax.ShapeDtypeStruct((m, D_MODEL), F32)]
    args = [h, gpre, gpost] + ([gnext] if gnext is not None else []) + list(wb)
    if gnext is not None:
        out_specs.append(tile)
        out_shape.append(jax.ShapeDtypeStruct((m, D_MODEL), BF16))
    if nxt is not None:
        f1, f3, f2, layer, idx = nxt
        dm, df = D_MODEL // n_i, D_FF // n_j
        in_specs += [
            pl.BlockSpec((None, None, dm, df), lambda i, j: (layer, idx, i, j)),
            pl.BlockSpec((None, None, dm, df), lambda i, j: (layer, idx, i, j)),
            pl.BlockSpec((None, None, df, dm), lambda i, j: (layer, idx, j, i)),
        ]
        out_specs += [
            pl.BlockSpec((dm, df), lambda i, j: (i, j)),
            pl.BlockSpec((dm, df), lambda i, j: (i, j)),
            pl.BlockSpec((df, dm), lambda i, j: (j, i)),
        ]
        out_shape += [jax.ShapeDtypeStruct((D_MODEL, D_FF), BF16), jax.ShapeDtypeStruct((D_MODEL, D_FF), BF16),
                      jax.ShapeDtypeStruct((D_FF, D_MODEL), BF16)]
        args += [f1, f3, f2]
    outs = pl.pallas_call(
        functools.partial(_ffn_kernel, cast_next=nxt is not None, emit_next=gnext is not None),
        grid=(n_i, n_j),
        in_specs=in_specs,
        out_specs=out_specs,
        out_shape=out_shape,
        scratch_shapes=[pltpu.VMEM((tm, D_MODEL), BF16)],
        compiler_params=_cparams(("parallel", "arbitrary")),
        name="ffn",
    )(*args)
    outs = list(outs)
    h_new = outs.pop(0)
    xn = outs.pop(0) if gnext is not None else None
    return h_new, xn, (tuple(outs) if nxt is not None else None)


def _inproj_kernel(x_ref, w_ref, o_ref):
    o_ref[...] = jnp.dot(x_ref[...], w_ref[...], preferred_element_type=F32)


def _inproj(xn, w, e, tn, tm=1024):
    m = xn.shape[0]
    n = w.shape[2]
    return pl.pallas_call(
        _inproj_kernel,
        grid=(m // tm, n // tn),
        in_specs=[
            pl.BlockSpec((tm, D_MODEL), lambda i, j: (i, 0)),
            pl.BlockSpec((None, D_MODEL, tn), lambda i, j: (e, 0, j)),
        ],
        out_specs=pl.BlockSpec((tm, tn), lambda i, j: (i, j)),
        out_shape=jax.ShapeDtypeStruct((m, n), F32),
        compiler_params=_cparams(("parallel", "parallel")),
        name="inproj",
    )(xn, w)


def _outproj_kernel(x1_ref, x2_ref, w_ref, h_ref, g_ref, o_ref):
    n1 = x1_ref.shape[1]
    m = jnp.dot(x1_ref[...], w_ref[0:n1, :].astype(BF16), preferred_element_type=F32)
    m = m + jnp.dot(x2_ref[...], w_ref[n1:, :].astype(BF16), preferred_element_type=F32)
    o_ref[...] = h_ref[...] + _rms(m, g_ref[...])


def _outproj(x1, x2, w, e, h, g, tm=512):
    m = h.shape[0]
    n1, n2 = x1.shape[1], x2.shape[1]
    return pl.pallas_call(
        _outproj_kernel,
        grid=(m // tm,),
        in_specs=[
            pl.BlockSpec((tm, n1), lambda i: (i, 0)),
            pl.BlockSpec((tm, n2), lambda i: (i, 0)),
            pl.BlockSpec((None, n1 + n2, D_MODEL), lambda i: (e, 0, 0)),
            pl.BlockSpec((tm, D_MODEL), lambda i: (i, 0)),
            pl.BlockSpec((1, D_MODEL), lambda i: (0, 0)),
        ],
        out_specs=pl.BlockSpec((tm, D_MODEL), lambda i: (i, 0)),
        out_shape=jax.ShapeDtypeStruct((m, D_MODEL), F32),
        compiler_params=_cparams(("parallel",)),
        name="outproj",
    )(x1, x2, w, h, g)


def _shift_rows(x, k):
    row = lax.broadcasted_iota(jnp.int32, x.shape, 0)
    return jnp.where(row >= k, pltpu.roll(x, k, axis=0), 0.0)


def _pool_kernel(p_ref, w_ref, s_ref, o_ref):
    gi = pl.program_id(1)
    x = p_ref[0]
    s2 = x + _shift_rows(x, 1)
    s4 = s2 + _shift_rows(s2, 2)
    s8 = s4 + _shift_rows(s4, 4)
    s16 = s8 + _shift_rows(s8, 8)
    wsum = jnp.where(gi == 0, s2, jnp.where(gi == 1, s4, jnp.where(gi == 2, s8, s16)))
    win = jnp.left_shift(2, gi)
    t = lax.broadcasted_iota(jnp.int32, (x.shape[0], 1), 0)
    cnt = jnp.minimum(t + 1, win).astype(F32)
    d = (wsum / cnt - x).astype(BF16)
    o_ref[0] = (jnp.dot(d, w_ref[0], preferred_element_type=F32) * s_ref[...]).astype(BF16)


def _pool(p3, pool_w, pool_scale):
    b, t, _ = p3.shape
    n_grp = len(POOL_WINDOWS)
    return pl.pallas_call(
        _pool_kernel,
        grid=(b, n_grp),
        in_specs=[
            pl.BlockSpec((1, t, POOL_GROUP_DIM), lambda i, g: (i, 0, g)),
            pl.BlockSpec((1, POOL_GROUP_DIM, POOL_GROUP_DIM), lambda i, g: (g, 0, 0)),
            pl.BlockSpec((1, POOL_GROUP_DIM), lambda i, g: (0, g)),
        ],
        out_specs=pl.BlockSpec((1, t, POOL_GROUP_DIM), lambda i, g: (i, 0, g)),
        out_shape=jax.ShapeDtypeStruct((b, t, POOL_DIM), BF16),
        compiler_params=_cparams(("parallel", "parallel")),
        name="pool",
    )(p3, pool_w, pool_scale)


def _rwprep_kernel(r_ref, k_ref, v_ref, l_ref, rp_ref, kp_ref, vp_ref, lp_ref,
                   mur_ref, muk_ref, muv_ref, mul_ref, w0_ref, w2_ref, a0_ref, a2_ref, g2_ref,
                   kk_ref, ka_ref, rk_ref,
                   ro_ref, cso_ref, ko_ref, vo_ref, ao_ref, bo_ref, go_ref, bonus_ref):
    ti = pl.program_id(1)

    def token_shift(cur_ref, prev_ref, mu_ref):
        x = cur_ref[0]
        prev_row = jnp.where(ti > 0, prev_ref[0, 7:8, :], 0.0)
        row = lax.broadcasted_iota(jnp.int32, x.shape, 0)
        xs = jnp.where(row == 0, prev_row, pltpu.roll(x, 1, axis=0))
        return x + (xs - x) * mu_ref[...]

    r = token_shift(r_ref, rp_ref, mur_ref)
    k = token_shift(k_ref, kp_ref, muk_ref)
    v = token_shift(v_ref, vp_ref, muv_ref)
    lo = token_shift(l_ref, lp_ref, mul_ref)

    lo_wa = lo[:, 0:128]
    xw = w0_ref[...] + jnp.dot(jnp.tanh(lo_wa).astype(BF16), w2_ref[...], preferred_element_type=F32)
    nx = -xw
    softplus = jnp.maximum(nx, 0.0) + jnp.log1p(jnp.exp(-jnp.abs(nx)))
    w_log = -softplus - 0.5
    lw = -jnp.exp(w_log)
    tt = lw.shape[0]
    ci = lax.broadcasted_iota(jnp.int32, (tt, tt), 0)
    cj = lax.broadcasted_iota(jnp.int32, (tt, tt), 1)
    chunk_tri = ((ci // SCAN_CHUNK == cj // SCAN_CHUNK) & (cj <= ci)).astype(BF16)
    cs = sum(jnp.dot(chunk_tri, term, preferred_element_type=F32) for term in _bf16_terms(lw, 3))
    cso_ref[0] = cs
    a = _sigmoid(a0_ref[...] + jnp.dot(lo_wa.astype(BF16), a2_ref[...], preferred_element_type=F32))
    go_ref[0] = jnp.dot(_sigmoid(lo[:, 128:384]).astype(BF16), g2_ref[...], preferred_element_type=F32)

    kk = k * kk_ref[...]
    kk = kk / jnp.maximum(jnp.sqrt(_head_sum(kk * kk)), 1e-12)
    k2 = k * (1.0 + (a - 1.0) * ka_ref[...])
    ro_ref[0] = r
    ko_ref[0] = k2
    vo_ref[0] = v
    ao_ref[0] = -kk * jnp.exp(cs - lw)
    bo_ref[0] = kk * a
    bonus_ref[0] = _head_sum(r * k2 * rk_ref[...]) * v


def _rwprep(p3, mu, w0, w2p, a0, a2p, g2p, kkp, kap, rkp, tt=256):
    b, t, _ = p3.shape
    nt = t // tt
    blk8 = tt // 8
    col = lambda c: (lambda i, j: (i, j, c))
    prev = lambda c: (lambda i, j: (i, jnp.maximum(j * blk8 - 1, 0), c))
    vec = lambda n: pl.BlockSpec((1, n), lambda i, j: (0, 0))
    lora_col = (POOL_DIM + 3 * RW_DIM) // LORA_PAD
    in_specs = [
        pl.BlockSpec((1, tt, RW_DIM), col(1)),
        pl.BlockSpec((1, tt, RW_DIM), col(2)),
        pl.BlockSpec((1, tt, RW_DIM), col(3)),
        pl.BlockSpec((1, tt, LORA_PAD), col(lora_col)),
        pl.BlockSpec((1, 8, RW_DIM), prev(1)),
        pl.BlockSpec((1, 8, RW_DIM), prev(2)),
        pl.BlockSpec((1, 8, RW_DIM), prev(3)),
        pl.BlockSpec((1, 8, LORA_PAD), prev(lora_col)),
        vec(RW_DIM), vec(RW_DIM), vec(RW_DIM), vec(LORA_PAD),
        vec(RW_DIM),
        pl.BlockSpec((128, RW_DIM), lambda i, j: (0, 0)),
        vec(RW_DIM),
        pl.BlockSpec((128, RW_DIM), lambda i, j: (0, 0)),
        pl.BlockSpec((256, RW_DIM), lambda i, j: (0, 0)),
        vec(RW_DIM), vec(RW_DIM), vec(RW_DIM),
    ]
    out_spec = pl.BlockSpec((1, tt, RW_DIM), lambda i, j: (i, j, 0))
    out_sds = jax.ShapeDtypeStruct((b, t, RW_DIM), F32)
    return pl.pallas_call(
        _rwprep_kernel,
        grid=(b, nt),
        in_specs=in_specs,
        out_specs=[out_spec] * 8,
        out_shape=[out_sds] * 8,
        compiler_params=_cparams(("parallel", "parallel")),
        name="rwprep",
    )(p3, p3, p3, p3, p3, p3, p3, p3, mu[0], mu[1], mu[2], mu[3], w0, w2p, a0, a2p, g2p, kkp, kap, rkp)


def _scan_kernel(r_ref, cs_ref, k_ref, v_ref, a_ref, b_ref, y_ref, h_ref):
    c_len, lanes = SCAN_CHUNK, SCAN_LANES

    @pl.when(pl.program_id(1) == 0)
    def _():
        h_ref[...] = jnp.zeros_like(h_ref)

    bmask_f = _head_mask(lanes)
    bmask = bmask_f.astype(BF16)
    bmask2 = jnp.concatenate([bmask, bmask], axis=1)
    row = lax.broadcasted_iota(jnp.int32, (c_len, lanes), 0)
    src = lax.broadcasted_iota(jnp.int32, (c_len, lanes), 1) % c_len
    strict, incl = src < row, src <= row
    ident = jnp.where(src == row, 1.0, 0.0)
    same = lambda n: (row // n) == (src // n)
    eye = (lax.broadcasted_iota(jnp.int32, (lanes, lanes), 0)
           == lax.broadcasted_iota(jnp.int32, (lanes, lanes), 1))

    def mm(x, y):
        return jnp.dot(x.astype(BF16), y.astype(BF16), preferred_element_type=F32)

    def stack(x):
        xb = jnp.concatenate([x.astype(BF16)] * SCAN_HEADS, axis=0)
        return xb * (bmask if x.shape[1] == lanes else bmask2)

    def apply(p, x):
        return jnp.dot(p.astype(BF16), stack(x), preferred_element_type=F32)

    def scores(x, y):
        return lax.dot_general(x.astype(BF16), stack(y), (((1,), (1,)), ((), ())), preferred_element_type=F32)

    n_grp = RW_DIM // lanes

    def chain(bi, g):
        sl = slice(g * lanes, (g + 1) * lanes)
        r, cs, k, v, a_t, b = (ref[bi, :, sl] for ref in (r_ref, cs_ref, k_ref, v_ref, a_ref, b_ref))
        tot = cs[c_len - 1:c_len, :]
        e_neg, e_rem = jnp.exp(-cs), jnp.exp(tot - cs)
        r_t = r * jnp.exp(cs)
        b_t, k_t = b * e_neg, k * e_neg
        b_h, k_h = b * e_rem, k * e_rem

        ar = jnp.concatenate([a_t, r_t], axis=0)
        sb, sk = scores(ar, b_t), scores(ar, k_t)
        p_ab = jnp.where(strict, sb[:c_len], 0.0)
        p_ak = jnp.where(strict, sk[:c_len], 0.0)
        p_rb = jnp.where(incl, sb[c_len:], 0.0)
        p_rk = jnp.where(incl, sk[c_len:], 0.0)
        yield

        a8 = jnp.where(same(8), p_ab, 0.0)
        e1 = ident + a8
        q2 = apply(a8, a8)
        xy = apply(jnp.concatenate([p_ak, p_rk], axis=0), v)
        x1, y2 = xy[:c_len], xy[c_len:]
        yield
        tq = apply(q2, jnp.concatenate([e1, q2], axis=1))
        t1 = e1 + tq[:, :lanes]
        yield
        t_inv = t1 + apply(tq[:, lanes:], t1)
        yield
        for n in (16, 32, 64):
            off = jnp.where(same(n) & jnp.logical_not(same(n // 2)), p_ab, 0.0)
            tmp = apply(off, t_inv)
            yield
            t_inv = t_inv + apply(t_inv, tmp)
            yield

        av = apply(t_inv, jnp.concatenate([a_t, x1], axis=1))
        yield
        ry = apply(p_rb, av)
        r2 = r_t + ry[:, :lanes]
        y2 = y2 + ry[:, lanes:]
        mn = mm(b_h.T, av)
        m_c = jnp.where(bmask_f, mn[:, :lanes], 0.0) + jnp.where(eye, jnp.exp(tot), 0.0)
        n_c = jnp.where(bmask_f, mn[:, lanes:] + mm(k_h.T, v), 0.0)
        yield

        h0 = h_ref[bi * n_grp + g]
        y_ref[bi, :, sl] = mm(r2, h0) + y2
        h_ref[bi * n_grp + g] = mm(m_c, h0) + n_c

    chains = [chain(bi, g) for bi in range(r_ref.shape[0]) for g in range(n_grp)]
    while all([next(c, False) is None for c in chains]):
        pass


def _scan(r, cs, k, v, a, b, nb=2):
    bsz, t, _ = r.shape
    spec = pl.BlockSpec((nb, SCAN_CHUNK, RW_DIM), lambda i, c: (i, c, 0))
    return pl.pallas_call(
        _scan_kernel,
        grid=(bsz // nb, t // SCAN_CHUNK),
        in_specs=[spec] * 6,
        out_specs=spec,
        out_shape=jax.ShapeDtypeStruct((bsz, t, RW_DIM), F32),
        scratch_shapes=[pltpu.VMEM((nb * (RW_DIM // SCAN_LANES), SCAN_LANES, SCAN_LANES), F32)],
        compiler_params=_cparams(("parallel", "arbitrary")),
        name="wkv7_scan",
    )(r, cs, k, v, a, b)


def _rwpost_kernel(y_ref, bonus_ref, g_ref, lng_ref, lnb_ref, o_ref):
    y = y_ref[...]
    ym = _head_sum(y) * (1.0 / RW_HEAD)
    yc = y - ym
    yv = _head_sum(yc * yc) * (1.0 / RW_HEAD)
    out = yc * lax.rsqrt(yv + RW_GN_EPS) * lng_ref[...] + lnb_ref[...] + bonus_ref[...]
    o_ref[...] = (out * g_ref[...]).astype(BF16)


def _rwpost(y, bonus, g, ln_g, ln_b, tt=512):
    m = y.shape[0]
    spec = pl.BlockSpec((tt, RW_DIM), lambda i: (i, 0))
    vec = pl.BlockSpec((1, RW_DIM), lambda i: (0, 0))
    return pl.pallas_call(
        _rwpost_kernel,
        grid=(m // tt,),
        in_specs=[spec, spec, spec, vec, vec],
        out_specs=spec,
        out_shape=jax.ShapeDtypeStruct((m, RW_DIM), BF16),
        compiler_params=_cparams(("parallel",)),
        name="rwpost",
    )(y, bonus, g, ln_g, ln_b)


def _conv_kernel(a_ref, gate_ref, dw_ref, db_ref, lng_ref, lnb_ref, o_ref, buf_ref, y_ref):
    tt = a_ref.shape[1]
    rb, cb = 128, 128

    @pl.when(pl.program_id(1) == 0)
    def _():
        buf_ref[0:CV_HALO, :] = jnp.zeros((CV_HALO, CV_DIM), F32)
        buf_ref[CV_HALO + tt:, :] = jnp.zeros((CV_TAIL, CV_DIM), F32)

    buf_ref[CV_HALO:CV_HALO + tt, :] = a_ref[0] * _sigmoid(gate_ref[0])
    first = CV_HALO - (CV_WIDTH - 1)
    ext = rb + 16
    for r0 in range(0, tt, rb):
        for c0 in range(0, CV_DIM, cb):
            acc = jnp.broadcast_to(db_ref[:, c0:c0 + cb], (rb, cb))
            for s in range(8):
                zs = None
                for j in range(s, CV_WIDTH, 8):
                    term = dw_ref[j:j + 1, c0:c0 + cb] * buf_ref[r0 + j - s:r0 + j - s + ext, c0:c0 + cb]
                    zs = term if zs is None else zs + term
                acc = acc + zs[first + s:first + s + rb]
            y_ref[r0:r0 + rb, c0:c0 + cb] = acc
    buf_ref[0:CV_HALO, :] = buf_ref[tt:tt + CV_HALO, :]
    z = _layer_norm(y_ref[...], lng_ref[...], lnb_ref[...])
    o_ref[0] = (z * _sigmoid(z)).astype(BF16)


def _conv(p3, dw, db, ln_g, ln_b, tt=256):
    b, t, _ = p3.shape
    vec = pl.BlockSpec((1, CV_DIM), lambda i, j: (0, 0))
    return pl.pallas_call(
        _conv_kernel,
        grid=(b, t // tt),
        in_specs=[
            pl.BlockSpec((1, tt, CV_DIM), lambda i, j: (i, j, 0)),
            pl.BlockSpec((1, tt, CV_DIM), lambda i, j: (i, j, 1)),
            pl.BlockSpec((CV_HALO, CV_DIM), lambda i, j: (0, 0)),
            vec, vec, vec,
        ],
        out_specs=pl.BlockSpec((1, tt, CV_DIM), lambda i, j: (i, j, 0)),
        out_shape=jax.ShapeDtypeStruct((b, t, CV_DIM), BF16),
        scratch_shapes=[pltpu.VMEM((CV_HALO + tt + CV_TAIL, CV_DIM), F32), pltpu.VMEM((tt, CV_DIM), F32)],
        compiler_params=_cparams(("parallel", "arbitrary")),
        name="conv",
    )(p3, p3, dw, db, ln_g, ln_b)


def _gmlp_kernel(u_ref, v_ref, lng_ref, lnb_ref, ws_ref, bs_ref, o_ref):
    tt = u_ref.shape[1]
    n_chunks = tt // SG_CHUNK
    grp = SG_DIM // SG_GROUPS
    u = _gelu(u_ref[0])
    v = _gelu(v_ref[0])
    vb = _layer_norm(v, lng_ref[...], lnb_ref[...]).astype(BF16)
    tri = (lax.broadcasted_iota(jnp.int32, (SG_CHUNK, SG_CHUNK), 1)
           <= lax.broadcasted_iota(jnp.int32, (SG_CHUNK, SG_CHUNK), 0))
    for g in range(SG_GROUPS):
        lanes = slice(g * grp, (g + 1) * grp)
        wg = jnp.where(tri, ws_ref[g], 0.0).astype(BF16)
        vg = jnp.concatenate([vb[c * SG_CHUNK:(c + 1) * SG_CHUNK, lanes] for c in range(n_chunks)], axis=1)
        sg = jnp.dot(wg, vg, preferred_element_type=F32)
        bias = bs_ref[:, g:g + 1]
        for c in range(n_chunks):
            rows = slice(c * SG_CHUNK, (c + 1) * SG_CHUNK)
            s = sg[:, c * grp:(c + 1) * grp] + bias
            o_ref[0, rows, lanes] = (u[rows, lanes] * s).astype(BF16)


def _gmlp(p3, ln_g, ln_b, ws, bs_t, tt=512):
    b, t, _ = p3.shape
    vec = pl.BlockSpec((1, SG_DIM), lambda i, j: (0, 0))
    return pl.pallas_call(
        _gmlp_kernel,
        grid=(b, t // tt),
        in_specs=[
            pl.BlockSpec((1, tt, SG_DIM), lambda i, j: (i, j, 2)),
            pl.BlockSpec((1, tt, SG_DIM), lambda i, j: (i, j, 3)),
            vec, vec,
            pl.BlockSpec((SG_GROUPS, SG_CHUNK, SG_CHUNK), lambda i, j: (0, 0, 0)),
            pl.BlockSpec((SG_CHUNK, SG_GROUPS), lambda i, j: (0, 0)),
        ],
        out_specs=pl.BlockSpec((1, tt, SG_DIM), lambda i, j: (i, j, 0)),
        out_shape=jax.ShapeDtypeStruct((b, t, SG_DIM), BF16),
        compiler_params=_cparams(("parallel", "parallel")),
        name="gmlp",
    )(p3, p3, ln_g, ln_b, ws, bs_t)


def _pad_rows(w, start, rows):
    return jnp.zeros((rows, w.shape[1]), BF16).at[start:start + w.shape[0]].set(w.astype(BF16))


def kernel(x, norm_g, ffn_w1, ffn_w3, ffn_w2, ev_w_in, ev_mu, pool_w, pool_scale, rw_w0, rw_w2, rw_a0, rw_a2,
           rw_g2, rw_kk, rw_ka, rw_rk, rw_ln_g, rw_ln_b, ev_w_out, od_w_in, cv_dw, cv_db, cv_ln_g, cv_ln_b,
           sg_ln_g, sg_ln_b, sg_ws, sg_b, od_w_out):
    bsz, t, d = x.shape
    m = bsz * t
    depth = norm_g.shape[0]
    row = lambda vct: vct.reshape(1, -1)

    wb = tuple(w[0, 0].astype(BF16) for w in (ffn_w1, ffn_w3, ffn_w2))
    ev_w_in_b = jnp.pad(ev_w_in.astype(BF16), ((0, 0), (0, 0), (0, EV_IN_PAD - EV_IN)))
    ev_w_out_b, od_w_in_b, od_w_out_b = ev_w_out, od_w_in.astype(BF16), od_w_out
    h = x.reshape(m, d)
    for layer in range(depth):
        g = norm_g[layer]
        h, xn, wb = _ffn(h, row(g[0]), row(g[1]), wb, nxt=(ffn_w1, ffn_w3, ffn_w2, layer, 1), gnext=row(g[2]),
                         tm=512, tf=512)
        if layer % 2 == 0:
            e = layer // 2
            p3 = _inproj(xn, ev_w_in_b, e, tn=EV_IN_PAD // 3).reshape(bsz, t, EV_IN_PAD)
            m1 = _pool(p3, pool_w[e].astype(BF16), row(pool_scale[e])).reshape(m, POOL_DIM)
            mu = ev_mu[e]
            mus = (row(mu[0:RW_DIM]), row(mu[RW_DIM:2 * RW_DIM]), row(mu[2 * RW_DIM:3 * RW_DIM]),
                   row(jnp.pad(mu[3 * RW_DIM:], (0, LORA_PAD - (EV_IN - POOL_DIM - 3 * RW_DIM)))))
            r, lw, k, v, a, b, gate, bonus = _rwprep(
                p3, mus, row(rw_w0[e]), _pad_rows(rw_w2[e], 0, 128), row(rw_a0[e]),
                _pad_rows(rw_a2[e], RW_DECAY_LORA, 128), _pad_rows(rw_g2[e], 0, 256),
                row(rw_kk[e]), row(rw_ka[e]), row(rw_rk[e]))
            y = _scan(r, lw, k, v, a, b)
            m2 = _rwpost(y.reshape(m, RW_DIM), bonus.reshape(m, RW_DIM), gate.reshape(m, RW_DIM),
                         row(rw_ln_g[e]), row(rw_ln_b[e]))
            h = _outproj(m1, m2, ev_w_out_b, e, h, row(g[3]))
        else:
            o = layer // 2
            p3 = _inproj(xn, od_w_in_b, o, tn=OD_IN // 4).reshape(bsz, t, OD_IN)
            dw = jnp.pad(cv_dw[o], ((0, CV_HALO - CV_WIDTH), (0, 0)))
            m1 = _conv(p3, dw, row(cv_db[o]), row(cv_ln_g[o]), row(cv_ln_b[o])).reshape(m, CV_DIM)
            m2 = _gmlp(p3, row(sg_ln_g[o]), row(sg_ln_b[o]), sg_ws[o], sg_b[o].T).reshape(m, SG_DIM)
            h = _outproj(m1, m2, od_w_out_b, o, h, row(g[3]))
        nxt = (ffn_w1, ffn_w3, ffn_w2, layer + 1, 0) if layer + 1 < depth else None
        h, _, wb = _ffn(h, row(g[4]), row(g[5]), wb, nxt=nxt)
    return h.reshape(bsz, t, d)
```
